```python
import math
import jax
import jax.numpy as jnp
from jax import lax
import numpy as np

D_MODEL = 1024
BATCH = 2
SEQ = 8192
DEPTH = 4
DEC_BATCH = 32
DEC_SEQ = 1
PAST_LEN = 8192
PAGE_SIZE = 128

N_A_LAYERS = DEPTH // 2
N_B_LAYERS = DEPTH - N_A_LAYERS
N_DENSE = (DEPTH + 1) // 2
N_MOE = DEPTH // 2

A_HEADS = 4
A_INNER = D_MODEL
A_HEAD_DIM = A_INNER // A_HEADS
A_CHUNK = 64
FORGET_BIAS = 3.0

B_HEADS = 8
B_HEAD_DIM = D_MODEL // (2 * B_HEADS)
B_QBLOCK = 128

D_FF = 2816
N_EXPERTS = 8
TOP_K = 2
D_FF_EXPERT = 2816

ALPHA = (2 * DEPTH) ** 0.25
BETA = (8 * DEPTH) ** -0.25
LN_EPS = 1e-5
F32 = jnp.float32

kernel_name = 'yoco_mlstm_diffattn_moe_step'


def layer_norm(x, g, b):
    xf = x.astype(F32)
    mu = jnp.mean(xf, axis=-1, keepdims=True)
    xc = xf - mu
    var = jnp.mean(xc * xc, axis=-1, keepdims=True)
    return (xc * lax.rsqrt(var + LN_EPS) * g.astype(F32) + b.astype(F32)).astype(x.dtype)


def rms_norm(x, g):
    xf = x.astype(F32)
    ms = jnp.mean(xf * xf, axis=-1, keepdims=True)
    return xf * lax.rsqrt(ms + LN_EPS) * g.astype(F32)


def head_layer_norm(h, g):
    mu = jnp.mean(h, axis=-1, keepdims=True)
    hc = h - mu
    var = jnp.mean(hc * hc, axis=-1, keepdims=True)
    return hc * lax.rsqrt(var + LN_EPS) * g[:, None, :]


def mlstm_chunkwise(q, k, v, log_i, log_f, c0, n0, m0):
    bsz, nh, t_len, _ = q.shape
    L = math.gcd(t_len, A_CHUNK)
    nc = t_len // L

    def to_chunks(a):
        return jnp.moveaxis(a.reshape(a.shape[:2] + (nc, L) + a.shape[3:]), 2, 0)

    causal = jnp.tril(jnp.ones((L, L), dtype=bool))

    def step(carry, inp):
        c, n, m = carry
        qc, kc, vc, ic, fc = inp
        b = jnp.cumsum(fc, axis=-1)
        dmat = jnp.where(causal, b[..., :, None] - b[..., None, :] + ic[..., None, :], -jnp.inf)
        inter = b + m[..., None]
        mt = jnp.maximum(inter, jnp.max(dmat, axis=-1))
        w_inter = jnp.exp(inter - mt)
        s = jnp.einsum('bhtd,bhsd->bhts', qc, kc) * jnp.exp(dmat - mt[..., None])
        num = w_inter[..., None] * jnp.einsum('bhtd,bhde->bhte', qc, c) + jnp.einsum('bhts,bhse->bhte', s, vc)
        den = w_inter * jnp.einsum('bhtd,bhd->bht', qc, n) + jnp.sum(s, axis=-1)
        h = num / jnp.maximum(jnp.abs(den), jnp.exp(-mt))[..., None]
        b_last = b[..., -1]
        g = b_last[..., None] - b + ic
        m_new = jnp.maximum(b_last + m, jnp.max(g, axis=-1))
        decay = jnp.exp(b_last + m - m_new)
        wk = jnp.exp(g - m_new[..., None])
        c_new = decay[..., None, None] * c + jnp.einsum('bhs,bhsd,bhse->bhde', wk, kc, vc)
        n_new = decay[..., None] * n + jnp.einsum('bhs,bhsd->bhd', wk, kc)
        return (c_new, n_new, m_new), h

    (c, n, m), hs = lax.scan(step, (c0, n0, m0), tuple(to_chunks(a) for a in (q, k, v, log_i, log_f)))
    h = jnp.moveaxis(hs, 0, 2).reshape(bsz, nh, t_len, -1)
    return h, c, n, m


def mlstm_mixer(x, w_in, b_gate, g_head, w_out, c0, n0, m0):
    bsz, t_len, _ = x.shape
    proj = x @ w_in
    q, k, v, o = jnp.split(proj[..., :4 * A_INNER], 4, axis=-1)
    gates = proj[..., 4 * A_INNER:].astype(F32) + b_gate.astype(F32)
    log_i = gates[..., :A_HEADS].transpose(0, 2, 1)
    log_f = jax.nn.log_sigmoid(gates[..., A_HEADS:]).transpose(0, 2, 1)

    def heads(a):
        return a.reshape(bsz, t_len, A_HEADS, A_HEAD_DIM).transpose(0, 2, 1, 3).astype(F32)

    h, c, n, m = mlstm_chunkwise(heads(q), heads(k) * (A_HEAD_DIM ** -0.5), heads(v), log_i, log_f,
                                 c0.astype(F32), n0.astype(F32), m0.astype(F32))
    h = head_layer_norm(h, g_head.astype(F32))
    h = h.transpose(0, 2, 1, 3).reshape(bsz, t_len, A_INNER).astype(x.dtype) * jax.nn.sigmoid(o)
    return h @ w_out, c, n, m


def shared_kv(h, w_kv):
    bsz, t_len, _ = h.shape
    kv = (h @ w_kv).reshape(bsz, t_len, 2, B_HEADS, 2 * B_HEAD_DIM)
    return kv[:, :, 0], kv[:, :, 1]


def diff_attention(q, k, v, q_offset, lam):
    bsz, tq = q.shape[:2]
    tk = k.shape[1]
    qb = math.gcd(tq, B_QBLOCK)
    nb = tq // qb
    k4 = k.reshape(bsz, tk, B_HEADS, 2, B_HEAD_DIM)
    k_pos = jnp.arange(tk)
    q_blocks = jnp.moveaxis(q.reshape(bsz, nb, qb, B_HEADS, 2, B_HEAD_DIM), 1, 0)
    starts = q_offset + jnp.arange(nb) * qb
    scale = B_HEAD_DIM ** -0.5

    def block(args):
        qblk, start = args
        s = jnp.einsum('bqhcd,bkhcd->bchqk', qblk, k4).astype(F32) * scale
        mask = k_pos[None, :] <= (start + jnp.arange(qb))[:, None]
        p = jax.nn.softmax(jnp.where(mask, s, -jnp.inf), axis=-1)
        a = p[:, 0] - lam * p[:, 1]
        return jnp.einsum('bhqk,bkhe->bqhe', a.astype(v.dtype), v)

    out = lax.map(block, (q_blocks, starts))
    return jnp.moveaxis(out, 0, 1).reshape(bsz, tq, B_HEADS, 2 * B_HEAD_DIM)


def diff_attn_mixer(x, k, v, q_offset, w_q, lam_p, g_subln, w_o, lam_init):
    bsz, t_len, _ = x.shape
    q = (x @ w_q).reshape(bsz, t_len, B_HEADS, 2 * B_HEAD_DIM)
    lp = lam_p.astype(F32)
    lam = jnp.exp(jnp.sum(lp[0] * lp[1])) - jnp.exp(jnp.sum(lp[2] * lp[3])) + lam_init
    o = diff_attention(q, k, v, q_offset, lam)
    o = rms_norm(o, g_subln) * (1.0 - lam_init)
    return o.reshape(bsz, t_len, B_HEADS * 2 * B_HEAD_DIM).astype(x.dtype) @ w_o


def swiglu(x, w_gu, w_down):
    g, u = jnp.split(x @ w_gu, 2, axis=-1)
    return (jax.nn.silu(g) * u) @ w_down


def moe_swiglu(x, w_router, w_gu, w_down):
    logits = (x @ w_router).astype(F32)
    top_v, top_i = lax.top_k(logits, TOP_K)
    gates = jax.nn.softmax(top_v, axis=-1)
    dense_gate = jnp.sum(jax.nn.one_hot(top_i, N_EXPERTS, dtype=F32) * gates[..., None], axis=-2)
    out = jnp.zeros_like(x)
    for e in range(N_EXPERTS):
        out = out + dense_gate[..., e:e + 1].astype(x.dtype) * swiglu(x, w_gu[e], w_down[e])
    return out


def lambda_init_fn(layer_idx):
    return 0.8 - 0.6 * math.exp(-0.3 * layer_idx)


def run_trunk(x, c0, n0, m0, k_past, v_past, q_offset, ln_g, ln_b, w_in_a, b_gate_a, g_head_a, w_out_a,
              w_kv, w_q_b, lam_b, g_subln_b, w_o_b, w_ffn_gu, w_ffn_down, w_router, w_moe_gu, w_moe_down):
    h = x
    cs, ns, ms = [], [], []
    k_new = v_new = k_all = v_all = None
    for i in range(DEPTH):
        if i < N_A_LAYERS:
            mix, c, n, m = mlstm_mixer(h, w_in_a[i], b_gate_a[i], g_head_a[i], w_out_a[i], c0[i], n0[i], m0[i])
            cs.append(c.astype(c0.dtype))
            ns.append(n.astype(n0.dtype))
            ms.append(m.astype(m0.dtype))
        else:
            j = i - N_A_LAYERS
            if j == 0:
                k_new, v_new = shared_kv(h, w_kv)
                if k_past is None:
                    k_all, v_all = k_new, v_new
                else:
                    k_all = jnp.concatenate([k_past.astype(k_new.dtype), k_new], axis=1)
                    v_all = jnp.concatenate([v_past.astype(v_new.dtype), v_new], axis=1)
            mix = diff_attn_mixer(h, k_all, v_all, q_offset, w_q_b[j], lam_b[j], g_subln_b[j], w_o_b[j],
                                  lambda_init_fn(i))
        h = layer_norm(ALPHA * h + mix, ln_g[i, 0], ln_b[i, 0])
        if i % 2 == 0:
            f = swiglu(h, w_ffn_gu[i // 2], w_ffn_down[i // 2])
        else:
            f = moe_swiglu(h, w_router[i // 2], w_moe_gu[i // 2], w_moe_down[i // 2])
        h = layer_norm(ALPHA * h + f, ln_g[i, 1], ln_b[i, 1])
    return h, jnp.stack(cs), jnp.stack(ns), jnp.stack(ms), k_new, v_new


def setup_inputs(seed: int = 0) -> dict:
    key = jax.random.key(seed)
    ks = jax.random.split(key, 24)
    n_pages = PAST_LEN // PAGE_SIZE
    n_used = DEC_BATCH * n_pages
    n_phys = (5 * n_used + 3) // 4
    kv_w = 2 * B_HEAD_DIM

    def nrm(k, shape, scale):
        return jax.random.normal(k, shape, F32) * scale

    gate_offset = jnp.concatenate([jnp.zeros((A_HEADS,), F32), jnp.full((A_HEADS,), FORGET_BIAS, F32)])
    return {
        'x_prompt': nrm(ks[0], (BATCH, SEQ, D_MODEL), 1.0),
        'x_sample': nrm(ks[1], (DEC_BATCH, DEC_SEQ, D_MODEL), 1.0),
        'state_c': nrm(ks[2], (N_A_LAYERS, DEC_BATCH, A_HEADS, A_HEAD_DIM, A_HEAD_DIM), 0.02),
        'state_n': nrm(ks[3], (N_A_LAYERS, DEC_BATCH, A_HEADS, A_HEAD_DIM), 0.1),
        'state_m': nrm(ks[4], (N_A_LAYERS, DEC_BATCH, A_HEADS), 1.0),
        'cache_k': nrm(ks[5], (n_phys, PAGE_SIZE, B_HEADS, kv_w), 1.0),
        'cache_v': nrm(ks[6], (n_phys, PAGE_SIZE, B_HEADS, kv_w), 1.0),
        'page_table': jax.random.permutation(ks[7], n_phys)[:n_used].reshape(DEC_BATCH, n_pages).astype(jnp.int32),
        'ln_g': 1.0 + nrm(ks[8], (DEPTH, 2, D_MODEL), 0.02),
        'ln_b': nrm(ks[9], (DEPTH, 2, D_MODEL), 0.02),
        'w_in_a': nrm(ks[10], (N_A_LAYERS, D_MODEL, 4 * A_INNER + 2 * A_HEADS), D_MODEL ** -0.5),
        'b_gate_a': nrm(ks[11], (N_A_LAYERS, 2 * A_HEADS), 0.1) + gate_offset,
        'g_head_a': 1.0 + nrm(ks[12], (N_A_LAYERS, A_HEADS, A_HEAD_DIM), 0.02),
        'w_out_a': nrm(ks[13], (N_A_LAYERS, A_INNER, D_MODEL), BETA * A_INNER ** -0.5),
        'w_kv': nrm(ks[14], (D_MODEL, 2 * B_HEADS * kv_w), D_MODEL ** -0.5),
        'w_q_b': nrm(ks[15], (N_B_LAYERS, D_MODEL, B_HEADS * kv_w), D_MODEL ** -0.5),
        'lam_b': nrm(ks[16], (N_B_LAYERS, 4, B_HEAD_DIM), 0.1),
        'g_subln_b': 1.0 + nrm(ks[17], (N_B_LAYERS, kv_w), 0.02),
        'w_o_b': nrm(ks[18], (N_B_LAYERS, B_HEADS * kv_w, D_MODEL), BETA * (B_HEADS * kv_w) ** -0.5),
        'w_ffn_gu': nrm(ks[19], (N_DENSE, D_MODEL, 2 * D_FF), D_MODEL ** -0.5),
        'w_ffn_down': nrm(ks[20], (N_DENSE, D_FF, D_MODEL), BETA * D_FF ** -0.5),
        'w_router': nrm(ks[21], (N_MOE, D_MODEL, N_EXPERTS), D_MODEL ** -0.5),
        'w_moe_gu': nrm(ks[22], (N_MOE, N_EXPERTS, D_MODEL, 2 * D_FF_EXPERT), D_MODEL ** -0.5),
        'w_moe_down': nrm(ks[23], (N_MOE, N_EXPERTS, D_FF_EXPERT, D_MODEL), BETA * D_FF_EXPERT ** -0.5),
    }


def reference(x_prompt, x_sample, state_c, state_n, state_m, cache_k, cache_v, page_table, ln_g, ln_b,
              w_in_a, b_gate_a, g_head_a, w_out_a, w_kv, w_q_b, lam_b, g_subln_b, w_o_b, w_ffn_gu,
              w_ffn_down, w_router, w_moe_gu, w_moe_down):
    bp = x_prompt.shape[0]
    zc = jnp.zeros((N_A_LAYERS, bp, A_HEADS, A_HEAD_DIM, A_HEAD_DIM), x_prompt.dtype)
    zn = jnp.zeros((N_A_LAYERS, bp, A_HEADS, A_HEAD_DIM), x_prompt.dtype)
    zm = jnp.zeros((N_A_LAYERS, bp, A_HEADS), x_prompt.dtype)
    y_prompt, c_p, n_p, m_p, k_p, v_p = run_trunk(
        x_prompt, zc, zn, zm, None, None, 0, ln_g, ln_b, w_in_a, b_gate_a, g_head_a, w_out_a,
        w_kv, w_q_b, lam_b, g_subln_b, w_o_b, w_ffn_gu, w_ffn_down, w_router, w_moe_gu, w_moe_down)

    bs = page_table.shape[0]
    k_past = cache_k[page_table].reshape(bs, -1, B_HEADS, 2 * B_HEAD_DIM)
    v_past = cache_v[page_table].reshape(bs, -1, B_HEADS, 2 * B_HEAD_DIM)
    past_len = k_past.shape[1]
    y_sample, c_s, n_s, m_s, k_s, v_s = run_trunk(
        x_sample, state_c, state_n, state_m, k_past, v_past, past_len, ln_g, ln_b, w_in_a, b_gate_a,
        g_head_a, w_out_a, w_kv, w_q_b, lam_b, g_subln_b, w_o_b, w_ffn_gu, w_ffn_down, w_router,
        w_moe_gu, w_moe_down)
    return (y_prompt, y_sample, c_p, n_p, m_p, k_p, v_p, c_s, n_s, m_s, k_s, v_s)
```

```python
import functools
import math

import jax
import jax.numpy as jnp
from jax import lax
from jax.experimental import pallas as pl
from jax.experimental.pallas import tpu as pltpu

F32 = jnp.float32
BF16 = jnp.bfloat16

DEPTH = 4
ALPHA = (2 * DEPTH) ** 0.25
LN_EPS = 1e-5
NEG = -1e30
LANES = 128
VMEM_LIMIT = 56 * 1024 * 1024

A_HEADS = 4
MLSTM_CHUNK = 256
ATTN_BLOCK = 512
N_EXPERTS = 8


def _lambda_init(layer_idx):
    return 0.8 - 0.6 * math.exp(-0.3 * layer_idx)


def _params(*sem):
    return pltpu.CompilerParams(dimension_semantics=sem, vmem_limit_bytes=VMEM_LIMIT)


def _dot(a, b):
    return jnp.dot(a, b, preferred_element_type=F32)


def _dot_nt(a, b):
    return lax.dot_general(a, b, (((1,), (1,)), ((), ())), preferred_element_type=F32)


def _dot_tn(a, b):
    return lax.dot_general(a, b, (((0,), (0,)), ((), ())), preferred_element_type=F32)


def _layer_norm(y, g, b):
    mu = jnp.mean(y, axis=-1, keepdims=True)
    yc = y - mu
    var = jnp.mean(yc * yc, axis=-1, keepdims=True)
    return yc * lax.rsqrt(var + LN_EPS) * g + b


def _sigmoid(x):
    return 1.0 / (1.0 + jnp.exp(-x))


def _row_tile(m, want):
    return want if m % want == 0 else m


def _mm_kernel(x_ref, w_ref, *o_refs):
    acc = _dot(x_ref[...], w_ref[...])
    for o_ref in o_refs:
        o_ref[...] = acc.astype(o_ref.dtype)


def matmul(x, w, out_dtypes, tm=1024, tn=1024):
    m, k = x.shape
    n = w.shape[1]
    tm = _row_tile(m, tm)
    tn = _row_tile(n, tn)
    return pl.pallas_call(
        _mm_kernel,
        grid=(m // tm, n // tn),
        in_specs=[pl.BlockSpec((tm, k), lambda i, j: (i, 0)),
                  pl.BlockSpec((k, tn), lambda i, j: (0, j))],
        out_specs=[pl.BlockSpec((tm, tn), lambda i, j: (i, j)) for _ in out_dtypes],
        out_shape=[jax.ShapeDtypeStruct((m, n), d) for d in out_dtypes],
        compiler_params=_params("parallel", "parallel"),
        name="matmul",
    )(x, w)


def _mm_res_ln_kernel(x_ref, w_ref, h_ref, g_ref, b_ref, of_ref, ob_ref):
    y = ALPHA * h_ref[...] + _dot(x_ref[...], w_ref[...])
    out = _layer_norm(y, g_ref[...], b_ref[...])
    of_ref[...] = out
    ob_ref[...] = out.astype(BF16)


def matmul_res_ln(x, w, h, g, b, tm=1024):
    m, k = x.shape
    d = w.shape[1]
    tm = _row_tile(m, tm)
    row = lambda i: (i, 0)
    fixed = lambda i: (0, 0)
    return pl.pallas_call(
        _mm_res_ln_kernel,
        grid=(m // tm,),
        in_specs=[pl.BlockSpec((tm, k), row), pl.BlockSpec((k, d), fixed), pl.BlockSpec((tm, d), row),
                  pl.BlockSpec((1, d), fixed), pl.BlockSpec((1, d), fixed)],
        out_specs=[pl.BlockSpec((tm, d), row), pl.BlockSpec((tm, d), row)],
        out_shape=[jax.ShapeDtypeStruct((m, d), F32), jax.ShapeDtypeStruct((m, d), BF16)],
        compiler_params=_params("parallel"),
        name="matmul_res_ln",
    )(x, w, h, g, b)


def _ffn_kernel(x_ref, h_ref, wg_ref, wu_ref, wd_ref, g_ref, b_ref, of_ref, ob_ref, acc_ref):
    f = pl.program_id(1)
    x = x_ref[...]
    gate = _dot(x, wg_ref[...])
    up = _dot(x, wu_ref[...])
    act = (gate * _sigmoid(gate) * up).astype(BF16)
    part = _dot(act, wd_ref[...])

    @pl.when(f == 0)
    def _():
        acc_ref[...] = part

    @pl.when(f > 0)
    def _():
        acc_ref[...] += part

    @pl.when(f == pl.num_programs(1) - 1)
    def _():
        out = _layer_norm(ALPHA * h_ref[...] + acc_ref[...], g_ref[...], b_ref[...])
        of_ref[...] = out
        ob_ref[...] = out.astype(BF16)


def ffn_res_ln(x, h, w_gu, w_down, g, b, tm=512, nf=2):
    m, d = x.shape
    ff = w_down.shape[0]
    tf = ff // nf
    tm = _row_tile(m, tm)
    row = lambda i, f: (i, 0)
    fixed = lambda i, f: (0, 0)
    return pl.pallas_call(
        _ffn_kernel,
        grid=(m // tm, nf),
        in_specs=[pl.BlockSpec((tm, d), row), pl.BlockSpec((tm, d), row),
                  pl.BlockSpec((d, tf), lambda i, f: (0, f)),
                  pl.BlockSpec((d, tf), lambda i, f: (0, nf + f)),
                  pl.BlockSpec((tf, d), lambda i, f: (f, 0)),
                  pl.BlockSpec((1, d), fixed), pl.BlockSpec((1, d), fixed)],
        out_specs=[pl.BlockSpec((tm, d), row), pl.BlockSpec((tm, d), row)],
        out_shape=[jax.ShapeDtypeStruct((m, d), F32), jax.ShapeDtypeStruct((m, d), BF16)],
        scratch_shapes=[pltpu.VMEM((tm, d), F32)],
        compiler_params=_params("parallel", "arbitrary"),
        name="ffn_res_ln",
    )(x, h, w_gu, w_gu, w_down, g, b)


def _router_kernel(x_ref, w_ref, dg_ref, *, n_experts):
    logits = jnp.dot(x_ref[...], w_ref[...], preferred_element_type=F32, precision=lax.Precision.HIGHEST)
    lane = lax.broadcasted_iota(jnp.int32, logits.shape, 1)
    lg = jnp.where(lane < n_experts, logits, -jnp.inf)
    v1 = jnp.max(lg, axis=1, keepdims=True)
    i1 = jnp.min(jnp.where(lg == v1, lane, LANES), axis=1, keepdims=True)
    lg2 = jnp.where(lane == i1, -jnp.inf, lg)
    v2 = jnp.max(lg2, axis=1, keepdims=True)
    i2 = jnp.min(jnp.where(lg2 == v2, lane, LANES), axis=1, keepdims=True)
    e = jnp.exp(v2 - v1)
    g1 = 1.0 / (1.0 + e)
    g2 = e / (1.0 + e)
    dg_ref[...] = jnp.where(lane == i1, g1, 0.0) + jnp.where(lane == i2, g2, 0.0)


def router(h, w_router, tm=1024):
    m, d = h.shape
    e = w_router.shape[1]
    w = jnp.pad(w_router, ((0, 0), (0, LANES - e)))
    tm = _row_tile(m, tm)
    return pl.pallas_call(
        functools.partial(_router_kernel, n_experts=e),
        grid=(m // tm,),
        in_specs=[pl.BlockSpec((tm, d), lambda i: (i, 0)), pl.BlockSpec((d, LANES), lambda i: (0, 0))],
        out_specs=pl.BlockSpec((tm, LANES), lambda i: (i, 0)),
        out_shape=jax.ShapeDtypeStruct((m, LANES), F32),
        compiler_params=_params("parallel"),
        name="router",
    )(h, w)


def _moe_dense_kernel(x_ref, h_ref, dg_ref, wg_ref, wu_ref, wd_ref, g_ref, b_ref, of_ref, ob_ref, acc_ref):
    e = pl.program_id(1)
    f = pl.program_id(2)
    x = x_ref[...]
    gate = _dot(x, wg_ref[...])
    up = _dot(x, wu_ref[...])
    act = (gate * _sigmoid(gate) * up).astype(BF16)
    dg = dg_ref[...]
    lane = lax.broadcasted_iota(jnp.int32, dg.shape, 1)
    w_e = jnp.sum(jnp.where(lane == e, dg, 0.0), axis=1, keepdims=True)
    part = w_e * _dot(act, wd_ref[...])
    first = jnp.logical_and(e == 0, f == 0)

    @pl.when(first)
    def _():
        acc_ref[...] = part

    @pl.when(jnp.logical_not(first))
    def _():
        acc_ref[...] += part

    @pl.when(jnp.logical_and(e == pl.num_programs(1) - 1, f == pl.num_programs(2) - 1))
    def _():
        out = _layer_norm(ALPHA * h_ref[...] + acc_ref[...], g_ref[...], b_ref[...])
        of_ref[...] = out
        ob_ref[...] = out.astype(BF16)


def moe_dense_res_ln(x, h, dg, w_gu, w_down, g, b, tm=512, nf=2):
    m, d = x.shape
    n_e, ff, _ = w_down.shape
    tf = ff // nf
    tm = _row_tile(m, tm)
    row = lambda i, e, f: (i, 0)
    fixed = lambda i, e, f: (0, 0)
    return pl.pallas_call(
        _moe_dense_kernel,
        grid=(m // tm, n_e, nf),
        in_specs=[pl.BlockSpec((tm, d), row), pl.BlockSpec((tm, d), row), pl.BlockSpec((tm, LANES), row),
                  pl.BlockSpec((None, d, tf), lambda i, e, f: (e, 0, f)),
                  pl.BlockSpec((None, d, tf), lambda i, e, f: (e, 0, nf + f)),
                  pl.BlockSpec((None, tf, d), lambda i, e, f: (e, f, 0)),
                  pl.BlockSpec((1, d), fixed), pl.BlockSpec((1, d), fixed)],
        out_specs=[pl.BlockSpec((tm, d), row), pl.BlockSpec((tm, d), row)],
        out_shape=[jax.ShapeDtypeStruct((m, d), F32), jax.ShapeDtypeStruct((m, d), BF16)],
        scratch_shapes=[pltpu.VMEM((tm, d), F32)],
        compiler_params=_params("parallel", "arbitrary", "arbitrary"),
        name="moe_dense_res_ln",
    )(x, h, dg, w_gu, w_gu, w_down, g, b)


def _log_sigmoid(x):
    return jnp.minimum(x, 0.0) - jnp.log(1.0 + jnp.exp(-jnp.abs(x)))


def _mlstm_kernel(q_ref, k_ref, v_ref, o_ref, gc_ref, gr_ref, bc_ref, br_ref, gh_ref, c0_ref, n0_ref, m0_ref,
                  hg_ref, c_ref, n_ref, m_ref, *, chunk, t_valid, nh, dh):
    ci = pl.program_id(1)

    @pl.when(ci == 0)
    def _():
        c_ref[...] = c0_ref[...]
        n_ref[...] = n0_ref[...]
        m_ref[...] = m0_ref[...]

    gc = gc_ref[0] + bc_ref[...]
    gr = gr_ref[0] + br_ref[...]
    li_c, lf_c = gc[:, :nh], _log_sigmoid(gc[:, nh:])
    li_r, lf_r = gr[:nh, :], _log_sigmoid(gr[nh:, :])
    if t_valid < chunk:
        tok_c = lax.broadcasted_iota(jnp.int32, li_c.shape, 0) < t_valid
        tok_r = lax.broadcasted_iota(jnp.int32, li_r.shape, 1) < t_valid
        li_c, lf_c = jnp.where(tok_c, li_c, NEG), jnp.where(tok_c, lf_c, 0.0)
        li_r, lf_r = jnp.where(tok_r, li_r, NEG), jnp.where(tok_r, lf_r, 0.0)
    row = lax.broadcasted_iota(jnp.int32, (chunk, chunk), 0)
    col = lax.broadcasted_iota(jnp.int32, (chunk, chunk), 1)
    causal = col <= row
    hi = lax.Precision.HIGHEST
    b_c = jnp.dot(causal.astype(F32), lf_c, preferred_element_type=F32, precision=hi)
    b_r = jnp.dot(lf_r, (row <= col).astype(F32), preferred_element_type=F32, precision=hi)

    for h in range(nh):
        hs = slice(h * dh, (h + 1) * dh)
        bcol, icol = b_c[:, h:h + 1], li_c[:, h:h + 1]
        brow, irow = b_r[h:h + 1, :], li_r[h:h + 1, :]
        m_prev = m_ref[0, h, :, 0:1]
        c_prev = c_ref[0, h]
        n_prev = n_ref[0, h]
        q = q_ref[:, hs]
        kf = k_ref[:, hs].astype(F32) * (dh ** -0.5)
        k = kf.astype(BF16)
        v = v_ref[:, hs]

        dmat = jnp.where(causal, bcol - brow + irow, NEG)
        inter = bcol + m_prev
        mt = jnp.maximum(inter, jnp.max(dmat, axis=1, keepdims=True))
        w_inter = jnp.exp(inter - mt)
        s = _dot_nt(q, k) * jnp.exp(dmat - mt)
        num = w_inter * _dot(q, c_prev.astype(BF16)) + _dot(s.astype(BF16), v)
        den = (w_inter * jnp.sum(q.astype(F32) * n_prev, axis=1, keepdims=True)
               + jnp.sum(s, axis=1, keepdims=True))
        hh = num / jnp.maximum(jnp.abs(den), jnp.exp(-mt))
        mu = jnp.mean(hh, axis=1, keepdims=True)
        hc = hh - mu
        var = jnp.mean(hc * hc, axis=1, keepdims=True)
        hn = hc * lax.rsqrt(var + LN_EPS) * gh_ref[h:h + 1, :]
        hg_ref[:, hs] = (hn * _sigmoid(o_ref[:, hs])).astype(BF16)

        b_last = bcol[chunk - 1:chunk, :]
        gdec = b_last - bcol + icol
        m_new = jnp.maximum(b_last + m_prev, jnp.max(gdec, axis=0, keepdims=True))
        decay = jnp.exp(b_last + m_prev - m_new)
        kw = kf * jnp.exp(gdec - m_new)
        c_ref[0, h] = decay * c_prev + _dot_tn(kw.astype(BF16), v)
        n_ref[0, h] = decay * n_prev + jnp.sum(kw, axis=0, keepdims=True)
        m_ref[0, h] = jnp.broadcast_to(m_new, (1, LANES))


def mlstm(qkv, o, gates, b_gate, g_head, c0, n0, m0, t_valid, chunk):
    bsz, t_len, d3 = qkv.shape
    d = d3 // 3
    nh = g_head.shape[0]
    dh = d // nh
    nc = t_len // chunk
    gates_t = jnp.swapaxes(gates, 1, 2)
    seq = lambda b, c: (b, 0, 0, 0)
    fixed = lambda b, c: (0, 0)
    kern = functools.partial(_mlstm_kernel, chunk=chunk, t_valid=t_valid, nh=nh, dh=dh)
    hg, c, n, m = pl.pallas_call(
        kern,
        grid=(bsz, nc),
        in_specs=[pl.BlockSpec((None, chunk, d), lambda b, c: (b, c, 0)),
                  pl.BlockSpec((None, chunk, d), lambda b, c: (b, c, 1)),
                  pl.BlockSpec((None, chunk, d), lambda b, c: (b, c, 2)),
                  pl.BlockSpec((None, chunk, d), lambda b, c: (b, c, 0)),
                  pl.BlockSpec((1, chunk, 2 * nh), lambda b, c: (b, c, 0)),
                  pl.BlockSpec((1, 2 * nh, chunk), lambda b, c: (b, 0, c)),
                  pl.BlockSpec((1, 2 * nh), fixed), pl.BlockSpec((2 * nh, 1), fixed),
                  pl.BlockSpec((nh, dh), fixed),
                  pl.BlockSpec((1, nh, dh, dh), seq), pl.BlockSpec((1, nh, 1, dh), seq),
                  pl.BlockSpec((1, nh, 1, LANES), seq)],
        out_specs=[pl.BlockSpec((None, chunk, d), lambda b, c: (b, c, 0)),
                   pl.BlockSpec((1, nh, dh, dh), seq), pl.BlockSpec((1, nh, 1, dh), seq),
                   pl.BlockSpec((1, nh, 1, LANES), seq)],
        out_shape=[jax.ShapeDtypeStruct((bsz, t_len, d), BF16),
                   jax.ShapeDtypeStruct((bsz, nh, dh, dh), F32),
                   jax.ShapeDtypeStruct((bsz, nh, 1, dh), F32),
                   jax.ShapeDtypeStruct((bsz, nh, 1, LANES), F32)],
        compiler_params=_params("parallel", "arbitrary"),
        name="mlstm",
    )(qkv, qkv, qkv, o, gates, gates_t, b_gate.reshape(1, 2 * nh), b_gate.reshape(2 * nh, 1), g_head,
      c0, n0.reshape(bsz, nh, 1, dh), jnp.broadcast_to(m0[:, :, None, None], (bsz, nh, 1, LANES)))
    return hg, c, n.reshape(bsz, nh, dh), m[:, :, 0, 0]


def _lambda_value(lam_ref, lam_init):
    lp = lam_ref[...]
    a = jnp.sum(lp[0:1, :] * lp[1:2, :], axis=1, keepdims=True)
    b = jnp.sum(lp[2:3, :] * lp[3:4, :], axis=1, keepdims=True)
    return jnp.exp(a) - jnp.exp(b) + lam_init


def _split_queries(q, dh):
    lane = lax.broadcasted_iota(jnp.int32, q.shape, 1)
    scale = dh ** -0.5
    zero = jnp.zeros_like(q)
    return jnp.concatenate([jnp.where(lane < dh, q, zero), jnp.where(lane >= dh, q, zero)], axis=0) * scale


def _diff_combine(acc, l, rows, lam, g, lam_init):
    o = acc[:rows] / l[:rows] - lam * (acc[rows:] / l[rows:])
    ms = jnp.mean(o * o, axis=1, keepdims=True)
    return o * lax.rsqrt(ms + LN_EPS) * g * (1.0 - lam_init)


def _flash_kernel(q_ref, k_ref, v_ref, lam_ref, g_ref, o_ref, m_ref, l_ref, acc_ref, *, blk, dh, lam_init):
    qi = pl.program_id(2)
    qs = _split_queries(q_ref[...], dh)
    m_ref[...] = jnp.full(m_ref.shape, NEG, F32)
    l_ref[...] = jnp.zeros(l_ref.shape, F32)
    acc_ref[...] = jnp.zeros(acc_ref.shape, F32)

    def block(ki, diagonal):
        start = pl.multiple_of(ki * blk, blk)
        k = k_ref[pl.ds(start, blk), :]
        v = v_ref[pl.ds(start, blk), :]
        s = _dot_nt(qs, k)
        if diagonal:
            row = lax.broadcasted_iota(jnp.int32, s.shape, 0) % blk
            col = lax.broadcasted_iota(jnp.int32, s.shape, 1)
            s = jnp.where(col <= row, s, NEG)
        m_prev = m_ref[...]
        m_new = jnp.maximum(m_prev, jnp.max(s, axis=1, keepdims=True))
        alpha = jnp.exp(m_prev - m_new)
        p = jnp.exp(s - m_new)
        l_ref[...] = alpha * l_ref[...] + jnp.sum(p, axis=1, keepdims=True)
        acc_ref[...] = alpha * acc_ref[...] + _dot(p.astype(BF16), v)
        m_ref[...] = m_new

    def body(ki, carry):
        block(ki, False)
        return carry

    lax.fori_loop(0, qi, body, 0)
    block(qi, True)
    lam = _lambda_value(lam_ref, lam_init)
    o_ref[...] = _diff_combine(acc_ref[...], l_ref[...], blk, lam, g_ref[...], lam_init).astype(o_ref.dtype)


def flash_diff_attention(q, k, v, lam_p, g_subln, lam_init, n_heads, blk=ATTN_BLOCK):
    bsz, t_len, d = q.shape
    hw = d // n_heads
    dh = hw // 2
    fixed = lambda b, h, i: (0, 0)
    return pl.pallas_call(
        functools.partial(_flash_kernel, blk=blk, dh=dh, lam_init=lam_init),
        grid=(bsz, n_heads, t_len // blk),
        in_specs=[pl.BlockSpec((None, blk, hw), lambda b, h, i: (b, i, h)),
                  pl.BlockSpec((None, t_len, hw), lambda b, h, i: (b, 0, h)),
                  pl.BlockSpec((None, t_len, hw), lambda b, h, i: (b, 0, h)),
                  pl.BlockSpec(lam_p.shape, fixed), pl.BlockSpec((1, hw), fixed)],
        out_specs=pl.BlockSpec((None, blk, hw), lambda b, h, i: (b, i, h)),
        out_shape=jax.ShapeDtypeStruct((bsz, t_len, d), BF16),
        scratch_shapes=[pltpu.VMEM((2 * blk, 1), F32), pltpu.VMEM((2 * blk, 1), F32),
                        pltpu.VMEM((2 * blk, hw), F32)],
        compiler_params=_params("parallel", "parallel", "arbitrary"),
        name="flash_diff_attention",
    )(q, k, v, lam_p, g_subln.reshape(1, hw))


def _decode_kernel(pt_ref, q_ref, k_ref, v_ref, kn_ref, vn_ref, lam_ref, g_ref, o_ref, qs_ref, m_ref, l_ref, acc_ref,
                   *, nh, dh, lam_init):
    del pt_ref
    j = pl.program_id(1)

    @pl.when(j == 0)
    def _():
        qs = _split_queries(q_ref[0], dh)
        qs_ref[...] = qs.astype(BF16)
        kn = jnp.concatenate([kn_ref[0, 0], kn_ref[0, 0]], axis=0)
        m_ref[...] = jnp.sum(qs * kn, axis=1, keepdims=True)
        l_ref[...] = jnp.ones(l_ref.shape, F32)
        acc_ref[...] = jnp.concatenate([vn_ref[0, 0], vn_ref[0, 0]], axis=0)

    page = k_ref.shape[1]
    k = k_ref[0].reshape(page * nh, 2 * dh).astype(BF16)
    v = v_ref[0].reshape(page * nh, 2 * dh).astype(BF16)
    s = _dot_nt(qs_ref[...], k)
    row = lax.broadcasted_iota(jnp.int32, s.shape, 0) % nh
    col = lax.broadcasted_iota(jnp.int32, s.shape, 1) % nh
    s = jnp.where(row == col, s, NEG)
    m_prev = m_ref[...]
    m_new = jnp.maximum(m_prev, jnp.max(s, axis=1, keepdims=True))
    alpha = jnp.exp(m_prev - m_new)
    p = jnp.exp(s - m_new)
    l_ref[...] = alpha * l_ref[...] + jnp.sum(p, axis=1, keepdims=True)
    acc_ref[...] = alpha * acc_ref[...] + _dot(p.astype(BF16), v)
    m_ref[...] = m_new

    @pl.when(j == pl.num_programs(1) - 1)
    def _():
        lam = _lambda_value(lam_ref, lam_init)
        o_ref[0] = _diff_combine(acc_ref[...], l_ref[...], nh, lam, g_ref[...], lam_init)


def decode_diff_attention(q, cache_k, cache_v, page_table, k_new, v_new, lam_p, g_subln, lam_init):
    n_seq, nh, hw = q.shape
    dh = hw // 2
    n_pages = page_table.shape[1]
    page = cache_k.shape[1]
    seq = lambda s, j, pt: (s, 0, 0)
    new = lambda s, j, pt: (s, 0, 0, 0)
    paged = lambda s, j, pt: (pt[s, j], 0, 0, 0)
    fixed = lambda s, j, pt: (0, 0)
    grid_spec = pltpu.PrefetchScalarGridSpec(
        num_scalar_prefetch=1,
        grid=(n_seq, n_pages),
        in_specs=[pl.BlockSpec((1, nh, hw), seq),
                  pl.BlockSpec((1, page, nh, hw), paged), pl.BlockSpec((1, page, nh, hw), paged),
                  pl.BlockSpec((1, 1, nh, hw), new), pl.BlockSpec((1, 1, nh, hw), new),
                  pl.BlockSpec(lam_p.shape, fixed), pl.BlockSpec((1, hw), fixed)],
        out_specs=pl.BlockSpec((1, nh, hw), seq),
        scratch_shapes=[pltpu.VMEM((2 * nh, hw), BF16), pltpu.VMEM((2 * nh, 1), F32),
                        pltpu.VMEM((2 * nh, 1), F32), pltpu.VMEM((2 * nh, hw), F32)],
    )
    return pl.pallas_call(
        functools.partial(_decode_kernel, nh=nh, dh=dh, lam_init=lam_init),
        grid_spec=grid_spec,
        out_shape=jax.ShapeDtypeStruct((n_seq, nh, hw), F32),
        compiler_params=_params("parallel", "arbitrary"),
        name="decode_diff_attention",
    )(page_table, q, cache_k, cache_v, k_new, v_new, lam_p, g_subln.reshape(1, hw))


def _trunk(x, state, past, wts):
    bsz, t_len, d = x.shape
    m_rows = bsz * t_len
    n_a = wts["w_qkv"].shape[0]
    hw = wts["g_subln"].shape[-1]
    n_heads = d // hw
    h32 = x.reshape(m_rows, d)
    hbf = h32.astype(BF16)
    cs, ns, ms = [], [], []
    k32 = v32 = kbf = vbf = None
    for i in range(DEPTH):
        ln_g, ln_b = wts["ln_g"][i], wts["ln_b"][i]
        if i < n_a:
            (qkv,) = matmul(hbf, wts["w_qkv"][i], [BF16])
            (o,) = matmul(hbf, wts["w_og"][i], [F32])
            chunk = MLSTM_CHUNK if t_len % MLSTM_CHUNK == 0 else 16
            t_pad = -(-t_len // chunk) * chunk
            gates = o[:, d:d + 2 * A_HEADS].reshape(bsz, t_len, 2 * A_HEADS)
            og = o[:, :d].reshape(bsz, t_len, d)
            qkv = qkv.reshape(bsz, t_len, 3 * d)
            if t_pad != t_len:
                pad = ((0, 0), (0, t_pad - t_len), (0, 0))
                qkv, og, gates = jnp.pad(qkv, pad), jnp.pad(og, pad), jnp.pad(gates, pad)
            if state is None:
                dh = d // A_HEADS
                c0 = jnp.zeros((bsz, A_HEADS, dh, dh), F32)
                n0 = jnp.zeros((bsz, A_HEADS, dh), F32)
                m0 = jnp.zeros((bsz, A_HEADS), F32)
            else:
                c0, n0, m0 = state[0][i], state[1][i], state[2][i]
            hg, c, n, m = mlstm(qkv, og, gates, wts["b_gate"][i], wts["g_head"][i], c0, n0, m0, t_len, chunk)
            cs.append(c)
            ns.append(n)
            ms.append(m)
            mix_in = hg[:, :t_len].reshape(m_rows, d)
            w_mix = wts["w_out"][i]
        else:
            j = i - n_a
            lam_init = _lambda_init(i)
            if j == 0:
                k32, kbf = matmul(hbf, wts["w_k"], [F32, BF16])
                v32, vbf = matmul(hbf, wts["w_v"], [F32, BF16])
            if past is None:
                (q,) = matmul(hbf, wts["w_q"][j], [BF16])
                o = flash_diff_attention(q.reshape(bsz, t_len, d), kbf.reshape(bsz, t_len, d),
                                         vbf.reshape(bsz, t_len, d), wts["lam"][j], wts["g_subln"][j],
                                         lam_init, n_heads)
                mix_in = o.reshape(m_rows, d)
            else:
                (q,) = matmul(hbf, wts["w_q"][j], [F32])
                cache_k, cache_v, page_table = past
                o = decode_diff_attention(q.reshape(bsz, n_heads, hw), cache_k, cache_v, page_table,
                                          k32.reshape(bsz, 1, n_heads, hw), v32.reshape(bsz, 1, n_heads, hw),
                                          wts["lam"][j], wts["g_subln"][j], lam_init)
                mix_in = o.reshape(m_rows, d).astype(BF16)
            w_mix = wts["w_o"][j]
        h32, hbf = matmul_res_ln(mix_in, w_mix, h32, ln_g[0:1], ln_b[0:1])
        if i % 2 == 0:
            h32, hbf = ffn_res_ln(hbf, h32, wts["w_ffn_gu"][i // 2], wts["w_ffn_down"][i // 2], ln_g[1:2], ln_b[1:2])
        else:
            dg = router(h32, wts["w_router"][i // 2])
            h32, hbf = moe_dense_res_ln(hbf, h32, dg, wts["w_moe_gu"][i // 2], wts["w_moe_down"][i // 2],
                                        ln_g[1:2], ln_b[1:2])
    y = h32.reshape(bsz, t_len, d)
    k_rows = k32.reshape(bsz, t_len, n_heads, hw)
    v_rows = v32.reshape(bsz, t_len, n_heads, hw)
    return y, jnp.stack(cs), jnp.stack(ns), jnp.stack(ms), k_rows, v_rows


def kernel(x_prompt, x_sample, state_c, state_n, state_m, cache_k, cache_v, page_table, ln_g, ln_b, w_in_a, b_gate_a, g_head_a, w_out_a, w_kv, w_q_b, lam_b, g_subln_b, w_o_b, w_ffn_gu, w_ffn_down, w_router, w_moe_gu, w_moe_down):
    d = x_prompt.shape[-1]
    n_gate = w_in_a.shape[-1] - 4 * d
    bf = lambda a: a.astype(BF16)
    w_og = jnp.pad(w_in_a[:, :, 3 * d:], ((0, 0), (0, 0), (0, LANES - n_gate)))
    wts = {
        "ln_g": ln_g, "ln_b": ln_b,
        "w_qkv": bf(w_in_a[:, :, :3 * d]), "w_og": bf(w_og), "b_gate": b_gate_a, "g_head": g_head_a,
        "w_out": bf(w_out_a), "w_k": bf(w_kv[:, :d]), "w_v": bf(w_kv[:, d:]), "w_q": bf(w_q_b),
        "lam": lam_b, "g_subln": g_subln_b, "w_o": bf(w_o_b), "w_ffn_gu": bf(w_ffn_gu),
        "w_ffn_down": bf(w_ffn_down), "w_router": w_router, "w_moe_gu": bf(w_moe_gu),
        "w_moe_down": bf(w_moe_down),
    }
    y_p, c_p, n_p, m_p, k_p, v_p = _trunk(x_prompt, None, None, wts)
    y_s, c_s, n_s, m_s, k_s, v_s = _trunk(x_sample, (state_c, state_n, state_m),
                                          (cache_k, cache_v, page_table), wts)
    return (y_p, y_s, c_p, n_p, m_p, k_p, v_p, c_s, n_s, m_s, k_s, v_s)
```

```python
import functools
import math

import jax
import jax.numpy as jnp
from jax import lax
from jax.experimental import pallas as pl
from jax.experimental.pallas import tpu as pltpu

F32 = jnp.float32
BF16 = jnp.bfloat16

DEPTH = 4
ALPHA = (2 * DEPTH) ** 0.25
LN_EPS = 1e-5
NEG = -1e30
LANES = 128
SUBLANES = 8
BF16_ROWS = 16
VMEM_LIMIT = 56 * 1024 * 1024

A_HEADS = 4
MLSTM_CHUNK = 256
ATTN_BLOCK = 512
ATTN_QSPLIT = 4
DECODE_PAGES = 8
MOE_BLOCK = 512
MOE_MIN_ROWS = 1024


def _lambda_init(layer_idx):
    return 0.8 - 0.6 * math.exp(-0.3 * layer_idx)


def _params(*sem):
    return pltpu.CompilerParams(dimension_semantics=sem, vmem_limit_bytes=VMEM_LIMIT)


def _contract(a, b, dims):
    precision = lax.Precision.HIGHEST if a.dtype == F32 and b.dtype == F32 else None
    return lax.dot_general(a, b, (dims, ((), ())), preferred_element_type=F32, precision=precision)


def _dot(a, b):
    return _contract(a, b, ((1,), (0,)))


def _dot_nt(a, b):
    return _contract(a, b, ((1,), (1,)))


def _dot_tn(a, b):
    return _contract(a, b, ((0,), (0,)))


def _layer_norm(y, g, b):
    mu = jnp.mean(y, axis=-1, keepdims=True)
    yc = y - mu
    var = jnp.mean(yc * yc, axis=-1, keepdims=True)
    return yc * lax.rsqrt(var + LN_EPS) * g + b


def _sigmoid(x):
    return 1.0 / (1.0 + jnp.exp(-x))


def _row_tile(m, want):
    return want if m % want == 0 else m


def _mm_kernel(x_ref, w_ref, *o_refs):
    acc = _dot(x_ref[...], w_ref[...])
    for o_ref in o_refs:
        o_ref[...] = acc.astype(o_ref.dtype)


def matmul(x, w, out_dtypes, tm=1024, tn=1024):
    m, k = x.shape
    n = w.shape[1]
    tm = _row_tile(m, tm)
    tn = _row_tile(n, tn)
    return pl.pallas_call(
        _mm_kernel,
        grid=(m // tm, n // tn),
        in_specs=[pl.BlockSpec((tm, k), lambda i, j: (i, 0)),
                  pl.BlockSpec((k, tn), lambda i, j: (0, j))],
        out_specs=[pl.BlockSpec((tm, tn), lambda i, j: (i, j)) for _ in out_dtypes],
        out_shape=[jax.ShapeDtypeStruct((m, n), d) for d in out_dtypes],
        compiler_params=_params("parallel", "parallel"),
        name="matmul",
    )(x, w)


def _mm_t_kernel(x_ref, wt_ref, o_ref):
    o_ref[...] = _dot_nt(wt_ref[...], x_ref[...]).astype(o_ref.dtype)


def matmul_t(x, wt, tm=1024, tn=1024):
    bsz, t_len, k = x.shape
    n = wt.shape[0]
    tm = _row_tile(t_len, tm)
    tn = _row_tile(n, tn)
    return pl.pallas_call(
        _mm_t_kernel,
        grid=(bsz, t_len // tm, n // tn),
        in_specs=[pl.BlockSpec((None, tm, k), lambda b, i, j: (b, i, 0)),
                  pl.BlockSpec((tn, k), lambda b, i, j: (j, 0))],
        out_specs=pl.BlockSpec((None, tn, tm), lambda b, i, j: (b, j, i)),
        out_shape=jax.ShapeDtypeStruct((bsz, n, t_len), BF16),
        compiler_params=_params("parallel", "parallel", "parallel"),
        name="matmul_t",
    )(x, wt)


def _mm_res_ln_kernel(x_ref, w_ref, h_ref, g_ref, b_ref, of_ref, ob_ref):
    y = ALPHA * h_ref[...] + _dot(x_ref[...], w_ref[...])
    out = _layer_norm(y, g_ref[...], b_ref[...])
    of_ref[...] = out
    ob_ref[...] = out.astype(BF16)


def matmul_res_ln(x, w, h, g, b, tm=1024):
    m, k = x.shape
    d = w.shape[1]
    tm = _row_tile(m, tm)
    row = lambda i: (i, 0)
    fixed = lambda i: (0, 0)
    return pl.pallas_call(
        _mm_res_ln_kernel,
        grid=(m // tm,),
        in_specs=[pl.BlockSpec((tm, k), row), pl.BlockSpec((k, d), fixed), pl.BlockSpec((tm, d), row),
                  pl.BlockSpec((1, d), fixed), pl.BlockSpec((1, d), fixed)],
        out_specs=[pl.BlockSpec((tm, d), row), pl.BlockSpec((tm, d), row)],
        out_shape=[jax.ShapeDtypeStruct((m, d), F32), jax.ShapeDtypeStruct((m, d), BF16)],
        compiler_params=_params("parallel"),
        name="matmul_res_ln",
    )(x, w, h, g, b)


def _swiglu_part(x, wg_ref, wu_ref, wd_ref):
    gate = _dot(x, wg_ref[...])
    up = _dot(x, wu_ref[...])
    act = (gate * _sigmoid(gate) * up).astype(wd_ref.dtype)
    return _dot(act, wd_ref[...])


def _ffn_kernel(x_ref, h_ref, wg_ref, wu_ref, wd_ref, g_ref, b_ref, of_ref, ob_ref, acc_ref):
    f = pl.program_id(1)
    part = _swiglu_part(x_ref[...], wg_ref, wu_ref, wd_ref)

    @pl.when(f == 0)
    def _():
        acc_ref[...] = part

    @pl.when(f > 0)
    def _():
        acc_ref[...] += part

    @pl.when(f == pl.num_programs(1) - 1)
    def _():
        out = _layer_norm(ALPHA * h_ref[...] + acc_ref[...], g_ref[...], b_ref[...])
        of_ref[...] = out
        ob_ref[...] = out.astype(BF16)


def ffn_res_ln(x, h, w_gu, w_down, g, b, tm=512, nf=2):
    m, d = x.shape
    ff = w_down.shape[0]
    tf = ff // nf
    tm = _row_tile(m, tm)
    row = lambda i, f: (i, 0)
    fixed = lambda i, f: (0, 0)
    return pl.pallas_call(
        _ffn_kernel,
        grid=(m // tm, nf),
        in_specs=[pl.BlockSpec((tm, d), row), pl.BlockSpec((tm, d), row),
                  pl.BlockSpec((d, tf), lambda i, f: (0, f)),
                  pl.BlockSpec((d, tf), lambda i, f: (0, nf + f)),
                  pl.BlockSpec((tf, d), lambda i, f: (f, 0)),
                  pl.BlockSpec((1, d), fixed), pl.BlockSpec((1, d), fixed)],
        out_specs=[pl.BlockSpec((tm, d), row), pl.BlockSpec((tm, d), row)],
        out_shape=[jax.ShapeDtypeStruct((m, d), F32), jax.ShapeDtypeStruct((m, d), BF16)],
        scratch_shapes=[pltpu.VMEM((tm, d), F32)],
        compiler_params=_params("parallel", "arbitrary"),
        name="ffn_res_ln",
    )(x, h, w_gu, w_gu, w_down, g, b)


def _router_kernel(x_ref, w_ref, dg_ref, meta_ref, cnt_ref, *, n_experts):
    logits = jnp.dot(x_ref[...], w_ref[...], preferred_element_type=F32, precision=lax.Precision.HIGHEST)
    lane = lax.broadcasted_iota(jnp.int32, logits.shape, 1)
    lg = jnp.where(lane < n_experts, logits, -jnp.inf)
    v1 = jnp.max(lg, axis=1, keepdims=True)
    i1 = jnp.min(jnp.where(lg == v1, lane, LANES), axis=1, keepdims=True)
    lg2 = jnp.where(lane == i1, -jnp.inf, lg)
    v2 = jnp.max(lg2, axis=1, keepdims=True)
    i2 = jnp.min(jnp.where(lg2 == v2, lane, LANES), axis=1, keepdims=True)
    e = jnp.exp(v2 - v1)
    g1 = 1.0 / (1.0 + e)
    g2 = e / (1.0 + e)
    pick1, pick2 = lane == i1, lane == i2
    dg_ref[...] = jnp.where(pick1, g1, 0.0) + jnp.where(pick2, g2, 0.0)

    rows = logits.shape[0]
    onehot = jnp.logical_or(pick1, pick2).astype(BF16)
    r = lax.broadcasted_iota(jnp.int32, (rows, rows), 0)
    c = lax.broadcasted_iota(jnp.int32, (rows, rows), 1)
    rank = _dot((c < r).astype(BF16), onehot)
    cnt = jnp.sum(onehot.astype(F32), axis=0, keepdims=True)
    padded = jnp.floor((cnt + (BF16_ROWS - 1)) * (1.0 / BF16_ROWS)) * BF16_ROWS
    er = lax.broadcasted_iota(jnp.int32, (LANES, LANES), 0)
    ec = lax.broadcasted_iota(jnp.int32, (LANES, LANES), 1)
    padded_rows = jnp.broadcast_to(padded, (BF16_ROWS, LANES)).astype(BF16)
    start = _dot(padded_rows, (er < ec).astype(BF16))[0:1, :]
    sorted_row = start + rank
    row1 = jnp.sum(jnp.where(pick1, sorted_row, 0.0), axis=1, keepdims=True)
    row2 = jnp.sum(jnp.where(pick2, sorted_row, 0.0), axis=1, keepdims=True)
    meta_ref[...] = (jnp.where(lane == 0, row1, 0.0) + jnp.where(lane == 1, row2, 0.0)
                     + jnp.where(lane == 2, g1, 0.0) + jnp.where(lane == 3, g2, 0.0))
    cnt_ref[0] = jnp.broadcast_to(cnt, (SUBLANES, LANES))


def router(h, w_router, tm=MOE_BLOCK):
    m, d = h.shape
    e = w_router.shape[1]
    w = jnp.pad(w_router, ((0, 0), (0, LANES - e)))
    tm = _row_tile(m, tm)
    row = lambda i: (i, 0)
    return pl.pallas_call(
        functools.partial(_router_kernel, n_experts=e),
        grid=(m // tm,),
        in_specs=[pl.BlockSpec((tm, d), row), pl.BlockSpec((d, LANES), lambda i: (0, 0))],
        out_specs=[pl.BlockSpec((tm, LANES), row), pl.BlockSpec((tm, LANES), row),
                   pl.BlockSpec((1, SUBLANES, LANES), lambda i: (i, 0, 0))],
        out_shape=[jax.ShapeDtypeStruct((m, LANES), F32), jax.ShapeDtypeStruct((m, LANES), F32),
                   jax.ShapeDtypeStruct((m // tm, SUBLANES, LANES), F32)],
        compiler_params=_params("parallel"),
        name="router",
    )(h, w)


def _moe_dense_kernel(x_ref, h_ref, dg_ref, wg_ref, wu_ref, wd_ref, g_ref, b_ref, of_ref, ob_ref, acc_ref):
    e = pl.program_id(1)
    f = pl.program_id(2)
    dg = dg_ref[...]
    lane = lax.broadcasted_iota(jnp.int32, dg.shape, 1)
    w_e = jnp.sum(jnp.where(lane == e, dg, 0.0), axis=1, keepdims=True)
    part = w_e * _swiglu_part(x_ref[...], wg_ref, wu_ref, wd_ref)
    first = jnp.logical_and(e == 0, f == 0)

    @pl.when(first)
    def _():
        acc_ref[...] = part

    @pl.when(jnp.logical_not(first))
    def _():
        acc_ref[...] += part

    @pl.when(jnp.logical_and(e == pl.num_programs(1) - 1, f == pl.num_programs(2) - 1))
    def _():
        out = _layer_norm(ALPHA * h_ref[...] + acc_ref[...], g_ref[...], b_ref[...])
        of_ref[...] = out
        ob_ref[...] = out.astype(BF16)


def moe_dense_res_ln(x, h, dg, w_gu, w_down, g, b, tm=512, nf=2):
    m, d = x.shape
    n_e, ff, _ = w_down.shape
    tf = ff // nf
    tm = _row_tile(m, tm)
    row = lambda i, e, f: (i, 0)
    fixed = lambda i, e, f: (0, 0)
    return pl.pallas_call(
        _moe_dense_kernel,
        grid=(m // tm, n_e, nf),
        in_specs=[pl.BlockSpec((tm, d), row), pl.BlockSpec((tm, d), row), pl.BlockSpec((tm, LANES), row),
                  pl.BlockSpec((None, d, tf), lambda i, e, f: (e, 0, f)),
                  pl.BlockSpec((None, d, tf), lambda i, e, f: (e, 0, nf + f)),
                  pl.BlockSpec((None, tf, d), lambda i, e, f: (e, f, 0)),
                  pl.BlockSpec((1, d), fixed), pl.BlockSpec((1, d), fixed)],
        out_specs=[pl.BlockSpec((tm, d), row), pl.BlockSpec((tm, d), row)],
        out_shape=[jax.ShapeDtypeStruct((m, d), F32), jax.ShapeDtypeStruct((m, d), BF16)],
        scratch_shapes=[pltpu.VMEM((tm, d), F32)],
        compiler_params=_params("parallel", "arbitrary", "arbitrary"),
        name="moe_dense_res_ln",
    )(x, h, dg, w_gu, w_gu, w_down, g, b)


def _moe_plan(cnt, n_chunks_max, n_tiles):
    nb, n_e = cnt.shape
    cpt = MOE_BLOCK // BF16_ROWS
    chunks = (cnt + BF16_ROWS - 1) // BF16_ROWS
    seg_local = jnp.cumsum(chunks, axis=1) - chunks
    n_chunks = jnp.sum(chunks, axis=1)
    tiles_e = (jnp.sum(chunks, axis=0) + cpt - 1) // cpt
    tile_end = jnp.cumsum(tiles_e)
    seg_global = ((tile_end - tiles_e) * cpt)[None, :] + jnp.cumsum(chunks, axis=0) - chunks
    c = jnp.arange(n_chunks_max, dtype=jnp.int32)
    e_of_c = jnp.sum(c[None, :, None] >= (seg_local + chunks)[:, None, :], axis=2)
    e_of_c = jnp.minimum(e_of_c, n_e - 1).astype(jnp.int32)
    dst = (jnp.take_along_axis(seg_global, e_of_c, axis=1) + c[None, :]
           - jnp.take_along_axis(seg_local, e_of_c, axis=1))
    tile_expert = jnp.sum(jnp.arange(n_tiles, dtype=jnp.int32)[:, None] >= tile_end[None, :], axis=1)
    tile_expert = jnp.minimum(tile_expert, n_e - 1).astype(jnp.int32)
    return (dst.reshape(-1).astype(jnp.int32), n_chunks.astype(jnp.int32), tile_expert,
            tile_end[-1:].astype(jnp.int32))


def _chunk_copy(src_ref, src_chunk, dst_ref, dst_chunk, sem):
    src = src_ref.at[pl.ds(pl.multiple_of(src_chunk * BF16_ROWS, BF16_ROWS), BF16_ROWS), :]
    dst = dst_ref.at[pl.ds(pl.multiple_of(dst_chunk * BF16_ROWS, BF16_ROWS), BF16_ROWS), :]
    return pltpu.make_async_copy(src, dst, sem)


def _dispatch_kernel(dst_ref, nch_ref, x_ref, rows_ref, buf_in_ref, xs_ref, sorted_ref, sem, *, n_chunks_max):
    del buf_in_ref
    b = pl.program_id(0)
    rows = rows_ref[0]
    n_sorted = sorted_ref.shape[0]
    r = lax.broadcasted_iota(jnp.int32, (n_sorted, rows.shape[1]), 0).astype(F32)
    sel = jnp.logical_or(r == rows[0:1, :], r == rows[1:2, :]).astype(BF16)
    sorted_ref[...] = _dot(sel, x_ref[...]).astype(BF16)
    n = nch_ref[b]

    def copy(c):
        return _chunk_copy(sorted_ref, c, xs_ref, dst_ref[b * n_chunks_max + c], sem)

    def start(c, carry):
        copy(c).start()
        return carry

    def wait(c, carry):
        copy(c).wait()
        return carry

    lax.fori_loop(0, n, start, 0)
    lax.fori_loop(0, n, wait, 0)


def _moe_ffn_kernel(te_ref, nt_ref, x_ref, wg_ref, wu_ref, wd_ref, o_ref, acc_ref):
    del te_ref
    i = pl.program_id(0)
    f = pl.program_id(1)

    @pl.when(i < nt_ref[0])
    def _():
        part = _swiglu_part(x_ref[...], wg_ref, wu_ref, wd_ref)

        @pl.when(f == 0)
        def _():
            acc_ref[...] = part

        @pl.when(f > 0)
        def _():
            acc_ref[...] += part

        @pl.when(f == pl.num_programs(1) - 1)
        def _():
            o_ref[...] = acc_ref[...].astype(o_ref.dtype)

    @pl.when(i >= nt_ref[0])
    def _():
        o_ref[...] = jnp.zeros(o_ref.shape, o_ref.dtype)


def _combine_kernel(dst_ref, nch_ref, ys_ref, meta_ref, h_ref, g_ref, b_ref, of_ref, ob_ref, sorted_ref, sem,
                    *, n_chunks_max):
    blk = pl.program_id(0)
    n = nch_ref[blk]

    def copy(c):
        return _chunk_copy(ys_ref, dst_ref[blk * n_chunks_max + c], sorted_ref, c, sem)

    def start(c, carry):
        copy(c).start()
        return carry

    def clear(c, carry):
        sorted_ref[pl.ds(pl.multiple_of(c * BF16_ROWS, BF16_ROWS), BF16_ROWS), :] = jnp.zeros(
            (BF16_ROWS, sorted_ref.shape[1]), sorted_ref.dtype)
        return carry

    def wait(c, carry):
        copy(c).wait()
        return carry

    lax.fori_loop(0, n, start, 0)
    lax.fori_loop(n, n_chunks_max, clear, 0)
    lax.fori_loop(0, n, wait, 0)
    meta = meta_ref[...]
    col = lax.broadcasted_iota(jnp.int32, (meta.shape[0], sorted_ref.shape[0]), 1).astype(F32)
    ys = sorted_ref[...]
    y = (meta[:, 2:3] * _dot((col == meta[:, 0:1]).astype(BF16), ys)
         + meta[:, 3:4] * _dot((col == meta[:, 1:2]).astype(BF16), ys))
    out = _layer_norm(ALPHA * h_ref[...] + y, g_ref[...], b_ref[...])
    of_ref[...] = out
    ob_ref[...] = out.astype(BF16)


def moe_sorted_res_ln(x, h, meta, cnt, w_gu, w_down, g, b, nf=2):
    m, d = x.shape
    n_e, ff, _ = w_down.shape
    tf = ff // nf
    nb = m // MOE_BLOCK
    n_chunks_max = 2 * MOE_BLOCK // BF16_ROWS + n_e
    n_sorted = n_chunks_max * BF16_ROWS
    cpt = MOE_BLOCK // BF16_ROWS
    n_tiles = -(-nb * n_chunks_max // cpt) + n_e
    dst, n_chunks, tile_expert, n_used = _moe_plan(cnt[:, 0, :n_e].astype(jnp.int32), n_chunks_max, n_tiles)
    rows_t = jnp.swapaxes(meta[:, :SUBLANES].reshape(nb, MOE_BLOCK, SUBLANES), 1, 2)

    any_space = pl.BlockSpec(memory_space=pl.ANY)
    xs = pl.pallas_call(
        functools.partial(_dispatch_kernel, n_chunks_max=n_chunks_max),
        grid_spec=pltpu.PrefetchScalarGridSpec(
            num_scalar_prefetch=2,
            grid=(nb,),
            in_specs=[pl.BlockSpec((MOE_BLOCK, d), lambda i, *_: (i, 0)),
                      pl.BlockSpec((1, SUBLANES, MOE_BLOCK), lambda i, *_: (i, 0, 0)),
                      any_space],
            out_specs=any_space,
            scratch_shapes=[pltpu.VMEM((n_sorted, d), BF16), pltpu.SemaphoreType.DMA(())],
        ),
        out_shape=jax.ShapeDtypeStruct((n_tiles * MOE_BLOCK, d), BF16),
        input_output_aliases={4: 0},
        compiler_params=_params("arbitrary"),
        name="moe_dispatch",
    )(dst, n_chunks, x, rows_t, jnp.zeros((n_tiles * MOE_BLOCK, d), BF16))

    def tile(i, f, te, nt):
        return jnp.minimum(i, nt[0] - 1)

    ys = pl.pallas_call(
        _moe_ffn_kernel,
        grid_spec=pltpu.PrefetchScalarGridSpec(
            num_scalar_prefetch=2,
            grid=(n_tiles, nf),
            in_specs=[pl.BlockSpec((MOE_BLOCK, d), lambda i, f, te, nt: (tile(i, f, te, nt), 0)),
                      pl.BlockSpec((None, d, tf), lambda i, f, te, nt: (te[tile(i, f, te, nt)], 0, f)),
                      pl.BlockSpec((None, d, tf), lambda i, f, te, nt: (te[tile(i, f, te, nt)], 0, nf + f)),
                      pl.BlockSpec((None, tf, d), lambda i, f, te, nt: (te[tile(i, f, te, nt)], f, 0))],
            out_specs=pl.BlockSpec((MOE_BLOCK, d), lambda i, f, te, nt: (i, 0)),
            scratch_shapes=[pltpu.VMEM((MOE_BLOCK, d), F32)],
        ),
        out_shape=jax.ShapeDtypeStruct((n_tiles * MOE_BLOCK, d), BF16),
        compiler_params=_params("arbitrary", "arbitrary"),
        name="moe_expert_ffn",
    )(tile_expert, n_used, xs, w_gu, w_gu, w_down)

    row = lambda i, *_: (i, 0)
    fixed = lambda i, *_: (0, 0)
    return pl.pallas_call(
        functools.partial(_combine_kernel, n_chunks_max=n_chunks_max),
        grid_spec=pltpu.PrefetchScalarGridSpec(
            num_scalar_prefetch=2,
            grid=(nb,),
            in_specs=[any_space, pl.BlockSpec((MOE_BLOCK, LANES), row), pl.BlockSpec((MOE_BLOCK, d), row),
                      pl.BlockSpec((1, d), fixed), pl.BlockSpec((1, d), fixed)],
            out_specs=[pl.BlockSpec((MOE_BLOCK, d), row), pl.BlockSpec((MOE_BLOCK, d), row)],
            scratch_shapes=[pltpu.VMEM((n_sorted, d), BF16), pltpu.SemaphoreType.DMA(())],
        ),
        out_shape=[jax.ShapeDtypeStruct((m, d), F32), jax.ShapeDtypeStruct((m, d), BF16)],
        compiler_params=_params("arbitrary"),
        name="moe_combine_res_ln",
    )(dst, n_chunks, ys, meta, h, g, b)


def _log_sigmoid(x):
    return jnp.minimum(x, 0.0) - jnp.log(1.0 + jnp.exp(-jnp.abs(x)))


def _mlstm_kernel(q_ref, k_ref, v_ref, o_ref, gc_ref, gr_ref, bc_ref, br_ref, gh_ref, c0_ref, n0_ref, m0_ref,
                  hg_ref, c_ref, n_ref, m_ref, *, chunk, t_valid, nh, dh):
    ci = pl.program_id(1)

    @pl.when(ci == 0)
    def _():
        c_ref[...] = c0_ref[...]
        n_ref[...] = n0_ref[...]
        m_ref[...] = m0_ref[...]

    gc = gc_ref[0] + bc_ref[...]
    gr = gr_ref[0] + br_ref[...]
    li_c, lf_c = gc[:, :nh], _log_sigmoid(gc[:, nh:])
    li_r, lf_r = gr[:nh, :], _log_sigmoid(gr[nh:, :])
    if t_valid < chunk:
        tok_c = lax.broadcasted_iota(jnp.int32, li_c.shape, 0) < t_valid
        tok_r = lax.broadcasted_iota(jnp.int32, li_r.shape, 1) < t_valid
        li_c, lf_c = jnp.where(tok_c, li_c, NEG), jnp.where(tok_c, lf_c, 0.0)
        li_r, lf_r = jnp.where(tok_r, li_r, NEG), jnp.where(tok_r, lf_r, 0.0)
    row = lax.broadcasted_iota(jnp.int32, (chunk, chunk), 0)
    col = lax.broadcasted_iota(jnp.int32, (chunk, chunk), 1)
    causal = col <= row
    hi = lax.Precision.HIGHEST
    b_c = jnp.dot(causal.astype(F32), lf_c, preferred_element_type=F32, precision=hi)
    b_r = jnp.dot(lf_r, (row <= col).astype(F32), preferred_element_type=F32, precision=hi)

    mm = q_ref.dtype
    for h in range(nh):
        hs = slice(h * dh, (h + 1) * dh)
        bcol, icol = b_c[:, h:h + 1], li_c[:, h:h + 1]
        brow, irow = b_r[h:h + 1, :], li_r[h:h + 1, :]
        m_prev = m_ref[0, h, :, 0:1]
        c_prev = c_ref[0, h]
        n_prev = n_ref[0, h]
        q = q_ref[:, hs]
        kf = k_ref[:, hs].astype(F32) * (dh ** -0.5)
        k = kf.astype(mm)
        v = v_ref[:, hs]

        dmat = jnp.where(causal, bcol - brow + irow, NEG)
        inter = bcol + m_prev
        mt = jnp.maximum(inter, jnp.max(dmat, axis=1, keepdims=True))
        w_inter = jnp.exp(inter - mt)
        s = _dot_nt(q, k) * jnp.exp(dmat - mt)
        num = w_inter * _dot(q, c_prev.astype(mm)) + _dot(s.astype(mm), v)
        den = (w_inter * jnp.sum(q.astype(F32) * n_prev, axis=1, keepdims=True)
               + jnp.sum(s, axis=1, keepdims=True))
        hh = num / jnp.maximum(jnp.abs(den), jnp.exp(-mt))
        mu = jnp.mean(hh, axis=1, keepdims=True)
        hc = hh - mu
        var = jnp.mean(hc * hc, axis=1, keepdims=True)
        hn = hc * lax.rsqrt(var + LN_EPS) * gh_ref[h:h + 1, :]
        hg_ref[:, hs] = (hn * _sigmoid(o_ref[:, hs])).astype(hg_ref.dtype)

        b_last = bcol[chunk - 1:chunk, :]
        gdec = b_last - bcol + icol
        m_new = jnp.maximum(b_last + m_prev, jnp.max(gdec, axis=0, keepdims=True))
        decay = jnp.exp(b_last + m_prev - m_new)
        kw = kf * jnp.exp(gdec - m_new)
        c_ref[0, h] = decay * c_prev + _dot_tn(kw.astype(mm), v)
        n_ref[0, h] = decay * n_prev + jnp.sum(kw, axis=0, keepdims=True)
        m_ref[0, h] = jnp.broadcast_to(m_new, (1, LANES))


def mlstm(qkv, o, gates, b_gate, g_head, c0, n0, m0, t_valid, chunk):
    bsz, t_len, d3 = qkv.shape
    d = d3 // 3
    nh = g_head.shape[0]
    dh = d // nh
    nc = t_len // chunk
    gates_t = jnp.swapaxes(gates, 1, 2)
    seq = lambda b, c: (b, 0, 0, 0)
    fixed = lambda b, c: (0, 0)
    kern = functools.partial(_mlstm_kernel, chunk=chunk, t_valid=t_valid, nh=nh, dh=dh)
    hg, c, n, m = pl.pallas_call(
        kern,
        grid=(bsz, nc),
        in_specs=[pl.BlockSpec((None, chunk, d), lambda b, c: (b, c, 0)),
                  pl.BlockSpec((None, chunk, d), lambda b, c: (b, c, 1)),
                  pl.BlockSpec((None, chunk, d), lambda b, c: (b, c, 2)),
                  pl.BlockSpec((None, chunk, d), lambda b, c: (b, c, 0)),
                  pl.BlockSpec((1, chunk, 2 * nh), lambda b, c: (b, c, 0)),
                  pl.BlockSpec((1, 2 * nh, chunk), lambda b, c: (b, 0, c)),
                  pl.BlockSpec((1, 2 * nh), fixed), pl.BlockSpec((2 * nh, 1), fixed),
                  pl.BlockSpec((nh, dh), fixed),
                  pl.BlockSpec((1, nh, dh, dh), seq), pl.BlockSpec((1, nh, 1, dh), seq),
                  pl.BlockSpec((1, nh, 1, LANES), seq)],
        out_specs=[pl.BlockSpec((None, chunk, d), lambda b, c: (b, c, 0)),
                   pl.BlockSpec((1, nh, dh, dh), seq), pl.BlockSpec((1, nh, 1, dh), seq),
                   pl.BlockSpec((1, nh, 1, LANES), seq)],
        out_shape=[jax.ShapeDtypeStruct((bsz, t_len, d), qkv.dtype),
                   jax.ShapeDtypeStruct((bsz, nh, dh, dh), F32),
                   jax.ShapeDtypeStruct((bsz, nh, 1, dh), F32),
                   jax.ShapeDtypeStruct((bsz, nh, 1, LANES), F32)],
        compiler_params=_params("parallel", "arbitrary"),
        name="mlstm",
    )(qkv, qkv, qkv, o, gates, gates_t, b_gate.reshape(1, 2 * nh), b_gate.reshape(2 * nh, 1), g_head,
      c0, n0.reshape(bsz, nh, 1, dh), jnp.broadcast_to(m0[:, :, None, None], (bsz, nh, 1, LANES)))
    return hg, c, n.reshape(bsz, nh, dh), m[:, :, 0, 0]


def _lambda_value(lam_ref, lam_init):
    lp = lam_ref[...]
    a = jnp.sum(lp[0:1, :] * lp[1:2, :], axis=1, keepdims=True)
    b = jnp.sum(lp[2:3, :] * lp[3:4, :], axis=1, keepdims=True)
    return jnp.exp(a) - jnp.exp(b) + lam_init


def _split_queries(q, dh):
    lane = lax.broadcasted_iota(jnp.int32, q.shape, 1)
    scale = dh ** -0.5
    zero = jnp.zeros_like(q)
    return jnp.concatenate([jnp.where(lane < dh, q, zero), jnp.where(lane >= dh, q, zero)], axis=0) * scale


def _flash_kernel(q_ref, k_ref, vt_ref, lam_ref, g_ref, o_ref, qs_ref, s_ref, m_ref, l_ref, acc_ref,
                  *, blk, dh, lam_init):
    qi = pl.program_id(2)
    qs_ref[...] = _split_queries(q_ref[...], dh)
    m_ref[...] = jnp.full(m_ref.shape, NEG, F32)
    l_ref[...] = jnp.zeros(l_ref.shape, F32)
    acc_ref[...] = jnp.zeros(acc_ref.shape, F32)
    sub = 2 * blk // ATTN_QSPLIT

    def scores(ki, c):
        k = k_ref[pl.ds(pl.multiple_of(ki * blk, blk), blk), :]
        return _dot_nt(k, qs_ref[c * sub:(c + 1) * sub, :])

    def update(ki, c, s, diagonal):
        cs = slice(c * sub, (c + 1) * sub)
        vt = vt_ref[:, pl.ds(pl.multiple_of(ki * blk, blk), blk)]
        if diagonal:
            key = lax.broadcasted_iota(jnp.int32, s.shape, 0)
            qry = (lax.broadcasted_iota(jnp.int32, s.shape, 1) + c * sub) % blk
            s = jnp.where(key <= qry, s, NEG)
        m_prev = m_ref[:, cs]
        m_new = jnp.maximum(m_prev, jnp.max(s, axis=0, keepdims=True))
        alpha = jnp.exp(m_prev - m_new)
        p = jnp.exp(s - m_new)
        l_ref[:, cs] = alpha * l_ref[:, cs] + jnp.sum(p, axis=0, keepdims=True)
        acc_ref[:, cs] = alpha * acc_ref[:, cs] + _dot(vt, p.astype(BF16))
        m_ref[:, cs] = m_new

    def block(ki, diagonal):
        s = s_ref[...]
        for c in range(ATTN_QSPLIT):
            if c + 1 < ATTN_QSPLIT:
                s_next = scores(ki, c + 1)
            elif not diagonal:
                s_next = scores(ki + 1, 0)
            else:
                s_next = None
            update(ki, c, s, diagonal)
            s = s_next
        if s is not None:
            s_ref[...] = s

    def body(ki, carry):
        block(ki, False)
        return carry

    s_ref[...] = scores(0, 0)
    lax.fori_loop(0, qi, body, 0)
    block(qi, True)
    lam = _lambda_value(lam_ref, lam_init)
    acc = acc_ref[...]
    l = l_ref[...]
    o = acc[:, :blk] / l[:, :blk] - lam * (acc[:, blk:] / l[:, blk:])
    ms = jnp.mean(o * o, axis=0, keepdims=True)
    o = o * lax.rsqrt(ms + LN_EPS) * (g_ref[...] * (1.0 - lam_init))
    o_ref[...] = o.T.astype(o_ref.dtype)


def flash_diff_attention(q, k, vt, lam_p, g_subln, lam_init, n_heads, blk=ATTN_BLOCK):
    bsz, t_len, d = q.shape
    hw = d // n_heads
    dh = hw // 2
    blk = min(blk, t_len)
    fixed = lambda b, h, i: (0, 0)
    return pl.pallas_call(
        functools.partial(_flash_kernel, blk=blk, dh=dh, lam_init=lam_init),
        grid=(bsz, n_heads, t_len // blk),
        in_specs=[pl.BlockSpec((None, blk, hw), lambda b, h, i: (b, i, h)),
                  pl.BlockSpec((None, t_len, hw), lambda b, h, i: (b, 0, h)),
                  pl.BlockSpec((None, hw, t_len), lambda b, h, i: (b, h, 0)),
                  pl.BlockSpec(lam_p.shape, fixed), pl.BlockSpec((hw, 1), fixed)],
        out_specs=pl.BlockSpec((None, blk, hw), lambda b, h, i: (b, i, h)),
        out_shape=jax.ShapeDtypeStruct((bsz, t_len, d), BF16),
        scratch_shapes=[pltpu.VMEM((2 * blk, hw), BF16), pltpu.VMEM((blk, 2 * blk // ATTN_QSPLIT), F32),
                        pltpu.VMEM((1, 2 * blk), F32), pltpu.VMEM((1, 2 * blk), F32),
                        pltpu.VMEM((hw, 2 * blk), F32)],
        compiler_params=_params("parallel", "parallel", "arbitrary"),
        name="flash_diff_attention",
    )(q, k, vt, lam_p, g_subln.reshape(hw, 1))


def _decode_kernel(pt_ref, q_ref, *refs, nh, dh, lam_init, pps):
    del pt_ref
    k_refs, v_refs = refs[:pps], refs[pps:2 * pps]
    kn_ref, vn_ref, lam_ref, g_ref, o_ref, qs_ref, m_ref, l_ref, acc_ref = refs[2 * pps:]
    j = pl.program_id(1)

    @pl.when(j == 0)
    def _():
        qs = _split_queries(q_ref[0], dh)
        qs_ref[...] = qs.astype(BF16)
        kn = jnp.concatenate([kn_ref[0, 0], kn_ref[0, 0]], axis=0)
        m_ref[...] = jnp.sum(qs * kn, axis=1, keepdims=True)
        l_ref[...] = jnp.ones(l_ref.shape, F32)
        acc_ref[...] = jnp.concatenate([vn_ref[0, 0], vn_ref[0, 0]], axis=0)

    page = k_refs[0].shape[1]
    qs = qs_ref[...]
    s = jnp.concatenate([_dot_nt(qs, k_ref[0].reshape(page * nh, 2 * dh).astype(BF16)) for k_ref in k_refs], axis=1)
    row = lax.broadcasted_iota(jnp.int32, s.shape, 0) % nh
    col = lax.broadcasted_iota(jnp.int32, s.shape, 1) % nh
    s = jnp.where(row == col, s, NEG)
    m_prev = m_ref[...]
    m_new = jnp.maximum(m_prev, jnp.max(s, axis=1, keepdims=True))
    alpha = jnp.exp(m_prev - m_new)
    p = jnp.exp(s - m_new).astype(BF16)
    l_ref[...] = alpha * l_ref[...] + jnp.sum(p.astype(F32), axis=1, keepdims=True)
    pv = _dot(p[:, :page * nh], v_refs[0][0].reshape(page * nh, 2 * dh).astype(BF16))
    for i in range(1, pps):
        pv += _dot(p[:, i * page * nh:(i + 1) * page * nh], v_refs[i][0].reshape(page * nh, 2 * dh).astype(BF16))
    acc_ref[...] = alpha * acc_ref[...] + pv
    m_ref[...] = m_new

    @pl.when(j == pl.num_programs(1) - 1)
    def _():
        lam = _lambda_value(lam_ref, lam_init)
        acc = acc_ref[...]
        l = l_ref[...]
        o = acc[:nh] / l[:nh] - lam * (acc[nh:] / l[nh:])
        ms = jnp.mean(o * o, axis=1, keepdims=True)
        o_ref[0] = o * lax.rsqrt(ms + LN_EPS) * g_ref[...] * (1.0 - lam_init)


def decode_diff_attention(q, cache_k, cache_v, page_table, k_new, v_new, lam_p, g_subln, lam_init):
    n_seq, nh, hw = q.shape
    dh = hw // 2
    n_pages = page_table.shape[1]
    page = cache_k.shape[1]
    pps = math.gcd(n_pages, DECODE_PAGES)
    seq = lambda s, j, pt: (s, 0, 0)
    new = lambda s, j, pt: (s, 0, 0, 0)
    fixed = lambda s, j, pt: (0, 0)

    def paged(i):
        return pl.BlockSpec((1, page, nh, hw), lambda s, j, pt: (pt[s, j * pps + i], 0, 0, 0))

    grid_spec = pltpu.PrefetchScalarGridSpec(
        num_scalar_prefetch=1,
        grid=(n_seq, n_pages // pps),
        in_specs=([pl.BlockSpec((1, nh, hw), seq)] + [paged(i) for i in range(pps)] * 2
                  + [pl.BlockSpec((1, 1, nh, hw), new), pl.BlockSpec((1, 1, nh, hw), new),
                     pl.BlockSpec(lam_p.shape, fixed), pl.BlockSpec((1, hw), fixed)]),
        out_specs=pl.BlockSpec((1, nh, hw), seq),
        scratch_shapes=[pltpu.VMEM((2 * nh, hw), BF16), pltpu.VMEM((2 * nh, 1), F32),
                        pltpu.VMEM((2 * nh, 1), F32), pltpu.VMEM((2 * nh, hw), F32)],
    )
    return pl.pallas_call(
        functools.partial(_decode_kernel, nh=nh, dh=dh, lam_init=lam_init, pps=pps),
        grid_spec=grid_spec,
        out_shape=jax.ShapeDtypeStruct((n_seq, nh, hw), F32),
        compiler_params=_params("parallel", "arbitrary"),
        name="decode_diff_attention",
    )(page_table, q, *([cache_k] * pps), *([cache_v] * pps), k_new, v_new, lam_p, g_subln.reshape(1, hw))


def _trunk(x, state, past, wts):
    bsz, t_len, d = x.shape
    m_rows = bsz * t_len
    n_a = wts["w_qkv"].shape[0]
    hw = wts["g_subln"].shape[-1]
    n_heads = d // hw
    act = wts["w_qkv"].dtype
    pick = (lambda h32, hbf: hbf) if act == BF16 else (lambda h32, hbf: h32)
    h32 = x.reshape(m_rows, d)
    ha = h32.astype(act)
    cs, ns, ms = [], [], []
    k32 = v32 = kbf = vt = None
    for i in range(DEPTH):
        ln_g, ln_b = wts["ln_g"][i], wts["ln_b"][i]
        if i < n_a:
            (qkv,) = matmul(ha, wts["w_qkv"][i], [act])
            (og,) = matmul(ha, wts["w_o_gate"][i], [F32])
            (gates,) = matmul(ha, wts["w_gates"][i], [F32])
            chunk = MLSTM_CHUNK if t_len % MLSTM_CHUNK == 0 else BF16_ROWS
            t_pad = -(-t_len // chunk) * chunk
            gates = gates[:, :2 * A_HEADS].reshape(bsz, t_len, 2 * A_HEADS)
            og = og.reshape(bsz, t_len, d)
            qkv = qkv.reshape(bsz, t_len, 3 * d)
            if t_pad != t_len:
                pad = ((0, 0), (0, t_pad - t_len), (0, 0))
                qkv, og, gates = jnp.pad(qkv, pad), jnp.pad(og, pad), jnp.pad(gates, pad)
            if state is None:
                dh = d // A_HEADS
                c0 = jnp.zeros((bsz, A_HEADS, dh, dh), F32)
                n0 = jnp.zeros((bsz, A_HEADS, dh), F32)
                m0 = jnp.zeros((bsz, A_HEADS), F32)
            else:
                c0, n0, m0 = state[0][i], state[1][i], state[2][i]
            hg, c, n, m = mlstm(qkv, og, gates, wts["b_gate"][i], wts["g_head"][i], c0, n0, m0, t_len, chunk)
            cs.append(c)
            ns.append(n)
            ms.append(m)
            mix_in = hg[:, :t_len].reshape(m_rows, d)
            w_mix = wts["w_out"][i]
        else:
            j = i - n_a
            lam_init = _lambda_init(i)
            if j == 0:
                if past is None:
                    k32, kbf = matmul(ha, wts["w_k"], [F32, BF16])
                    (v32,) = matmul(ha, wts["w_v"], [F32])
                    vt = matmul_t(ha.reshape(bsz, t_len, d), wts["w_v_t"])
                else:
                    (k32,) = matmul(ha, wts["w_k"], [F32])
                    (v32,) = matmul(ha, wts["w_v"], [F32])
            if past is None:
                (q,) = matmul(ha, wts["w_q"][j], [BF16])
                o = flash_diff_attention(q.reshape(bsz, t_len, d), kbf.reshape(bsz, t_len, d), vt,
                                         wts["lam"][j], wts["g_subln"][j], lam_init, n_heads)
                mix_in = o.reshape(m_rows, d)
            else:
                (q,) = matmul(ha, wts["w_q"][j], [F32])
                cache_k, cache_v, page_table = past
                o = decode_diff_attention(q.reshape(bsz, n_heads, hw), cache_k, cache_v, page_table,
                                          k32.reshape(bsz, 1, n_heads, hw), v32.reshape(bsz, 1, n_heads, hw),
                                          wts["lam"][j], wts["g_subln"][j], lam_init)
                mix_in = o.reshape(m_rows, d).astype(act)
            w_mix = wts["w_o"][j]
        h32, hbf = matmul_res_ln(mix_in, w_mix, h32, ln_g[0:1], ln_b[0:1])
        ha = pick(h32, hbf)
        if i % 2 == 0:
            h32, hbf = ffn_res_ln(ha, h32, wts["w_ffn_gu"][i // 2], wts["w_ffn_down"][i // 2], ln_g[1:2], ln_b[1:2])
        else:
            dg, meta, cnt = router(h32, wts["w_router"][i // 2])
            w_gu, w_down = wts["w_moe_gu"][i // 2], wts["w_moe_down"][i // 2]
            if m_rows >= MOE_MIN_ROWS and m_rows % MOE_BLOCK == 0 and act == BF16:
                h32, hbf = moe_sorted_res_ln(ha, h32, meta, cnt, w_gu, w_down, ln_g[1:2], ln_b[1:2])
            else:
                h32, hbf = moe_dense_res_ln(ha, h32, dg, w_gu, w_down, ln_g[1:2], ln_b[1:2])
        ha = pick(h32, hbf)
    y = h32.reshape(bsz, t_len, d)
    k_rows = k32.reshape(bsz, t_len, n_heads, hw)
    v_rows = v32.reshape(bsz, t_len, n_heads, hw)
    return y, jnp.stack(cs), jnp.stack(ns), jnp.stack(ms), k_rows, v_rows


def kernel(x_prompt, x_sample, state_c, state_n, state_m, cache_k, cache_v, page_table, ln_g, ln_b, w_in_a, b_gate_a, g_head_a, w_out_a, w_kv, w_q_b, lam_b, g_subln_b, w_o_b, w_ffn_gu, w_ffn_down, w_router, w_moe_gu, w_moe_down):
    d = x_prompt.shape[-1]
    n_gate = w_in_a.shape[-1] - 4 * d
    w_gates = jnp.pad(w_in_a[:, :, 4 * d:], ((0, 0), (0, 0), (0, LANES - n_gate)))

    def weights(cast):
        return {
            "ln_g": ln_g, "ln_b": ln_b, "b_gate": b_gate_a, "g_head": g_head_a, "lam": lam_b,
            "g_subln": g_subln_b, "w_router": w_router,
            "w_qkv": cast(w_in_a[:, :, :3 * d]), "w_o_gate": cast(w_in_a[:, :, 3 * d:4 * d]),
            "w_gates": cast(w_gates), "w_out": cast(w_out_a), "w_k": cast(w_kv[:, :d]), "w_v": cast(w_kv[:, d:]),
            "w_v_t": cast(w_kv[:, d:].T), "w_q": cast(w_q_b), "w_o": cast(w_o_b), "w_ffn_gu": cast(w_ffn_gu),
            "w_ffn_down": cast(w_ffn_down), "w_moe_gu": cast(w_moe_gu), "w_moe_down": cast(w_moe_down),
        }

    y_p, c_p, n_p, m_p, k_p, v_p = _trunk(x_prompt, None, None, weights(lambda a: a.astype(BF16)))
    y_s, c_s, n_s, m_s, k_s, v_s = _trunk(x_sample, (state_c, state_n, state_m),
                                          (cache_k, cache_v, page_table), weights(lambda a: a))
    return (y_p, y_s, c_p, n_p, m_p, k_p, v_p, c_s, n_s, m_s, k_s, v_s)
```

```python
import functools
import math

import jax
import jax.numpy as jnp
from jax import lax
from jax.experimental import pallas as pl
from jax.experimental.pallas import tpu as pltpu

F32 = jnp.float32
BF16 = jnp.bfloat16

DEPTH = 4
ALPHA = (2 * DEPTH) ** 0.25
LN_EPS = 1e-5
NEG = -1e30
LANES = 128
SUBLANES = 8
BF16_ROWS = 16
VMEM_LIMIT = 56 * 1024 * 1024

A_HEADS = 4
MLSTM_CHUNK = 256
ATTN_BLOCK = 512
ATTN_QSPLIT = 2
ATTN_LOOKAHEAD = 2
DECODE_PAGES = 8
MOE_BLOCK = 512
MOE_MIN_ROWS = 1024


def _lambda_init(layer_idx):
    return 0.8 - 0.6 * math.exp(-0.3 * layer_idx)


def _params(*sem):
    return pltpu.CompilerParams(dimension_semantics=sem, vmem_limit_bytes=VMEM_LIMIT)


def _contract(a, b, dims):
    precision = lax.Precision.HIGHEST if a.dtype == F32 and b.dtype == F32 else None
    return lax.dot_general(a, b, (dims, ((), ())), preferred_element_type=F32, precision=precision)


def _dot(a, b):
    return _contract(a, b, ((1,), (0,)))


def _dot_nt(a, b):
    return _contract(a, b, ((1,), (1,)))


def _dot_tn(a, b):
    return _contract(a, b, ((0,), (0,)))


def _layer_norm(y, g, b):
    mu = jnp.mean(y, axis=-1, keepdims=True)
    yc = y - mu
    var = jnp.mean(yc * yc, axis=-1, keepdims=True)
    return yc * lax.rsqrt(var + LN_EPS) * g + b


def _sigmoid(x):
    return 1.0 / (1.0 + jnp.exp(-x))


def _row_tile(m, want):
    return want if m % want == 0 else m


def _wspec(lead, block, index):
    return pl.BlockSpec((None,) * len(lead) + block, lambda *a: tuple(lead) + index(*a))


def _mm_kernel(x_ref, w_ref, *o_refs, scale):
    acc = _dot(x_ref[...], w_ref[...])
    if scale != 1.0:
        acc = acc * scale
    for o_ref in o_refs:
        o_ref[...] = acc.astype(o_ref.dtype)


def matmul(x, w, out_dtypes, tm=1024, tn=1024, scale=1.0, lead=(), col0=0, n=None):
    m, k = x.shape
    n = w.shape[-1] if n is None else n
    tm = _row_tile(m, tm)
    tn = _row_tile(n, tn)
    return pl.pallas_call(
        functools.partial(_mm_kernel, scale=scale),
        grid=(m // tm, n // tn),
        in_specs=[pl.BlockSpec((tm, k), lambda i, j: (i, 0)),
                  _wspec(lead, (k, tn), lambda i, j: (0, col0 + j))],
        out_specs=[pl.BlockSpec((tm, tn), lambda i, j: (i, j)) for _ in out_dtypes],
        out_shape=[jax.ShapeDtypeStruct((m, n), d) for d in out_dtypes],
        compiler_params=_params("parallel", "parallel"),
        name="matmul",
    )(x, w)


def _mm_t_kernel(x_ref, wt_ref, o_ref):
    o_ref[...] = _dot_nt(wt_ref[...], x_ref[...]).astype(o_ref.dtype)


def matmul_t(x, wt, tm=1024, tn=1024):
    bsz, t_len, k = x.shape
    n = wt.shape[0]
    tm = _row_tile(t_len, tm)
    tn = _row_tile(n, tn)
    return pl.pallas_call(
        _mm_t_kernel,
        grid=(bsz, t_len // tm, n // tn),
        in_specs=[pl.BlockSpec((None, tm, k), lambda b, i, j: (b, i, 0)),
                  pl.BlockSpec((tn, k), lambda b, i, j: (j, 0))],
        out_specs=pl.BlockSpec((None, tn, tm), lambda b, i, j: (b, j, i)),
        out_shape=jax.ShapeDtypeStruct((bsz, n, t_len), BF16),
        compiler_params=_params("parallel", "parallel", "parallel"),
        name="matmul_t",
    )(x, wt)


def _mm_res_ln_kernel(x_ref, w_ref, h_ref, g_ref, b_ref, of_ref, ob_ref):
    y = ALPHA * h_ref[...] + _dot(x_ref[...], w_ref[...])
    out = _layer_norm(y, g_ref[...], b_ref[...])
    of_ref[...] = out
    ob_ref[...] = out.astype(BF16)


def matmul_res_ln(x, w, h, g, b, tm=1024, lead=()):
    m, k = x.shape
    d = w.shape[-1]
    tm = _row_tile(m, tm)
    row = lambda i: (i, 0)
    fixed = lambda i: (0, 0)
    return pl.pallas_call(
        _mm_res_ln_kernel,
        grid=(m // tm,),
        in_specs=[pl.BlockSpec((tm, k), row), _wspec(lead, (k, d), fixed), pl.BlockSpec((tm, d), row),
                  pl.BlockSpec((1, d), fixed), pl.BlockSpec((1, d), fixed)],
        out_specs=[pl.BlockSpec((tm, d), row), pl.BlockSpec((tm, d), row)],
        out_shape=[jax.ShapeDtypeStruct((m, d), F32), jax.ShapeDtypeStruct((m, d), BF16)],
        compiler_params=_params("parallel"),
        name="matmul_res_ln",
    )(x, w, h, g, b)


def _swiglu_part(x, wg_ref, wu_ref, wd_ref):
    gate = _dot(x, wg_ref[...])
    up = _dot(x, wu_ref[...])
    act = (gate * _sigmoid(gate) * up).astype(wd_ref.dtype)
    return _dot(act, wd_ref[...])


def _ffn_kernel(x_ref, h_ref, wg_ref, wu_ref, wd_ref, g_ref, b_ref, of_ref, ob_ref, acc_ref):
    f = pl.program_id(1)
    part = _swiglu_part(x_ref[...], wg_ref, wu_ref, wd_ref)

    @pl.when(f == 0)
    def _():
        acc_ref[...] = part

    @pl.when(f > 0)
    def _():
        acc_ref[...] += part

    @pl.when(f == pl.num_programs(1) - 1)
    def _():
        out = _layer_norm(ALPHA * h_ref[...] + acc_ref[...], g_ref[...], b_ref[...])
        of_ref[...] = out
        ob_ref[...] = out.astype(BF16)


def ffn_res_ln(x, h, w_gu, w_down, g, b, tm=512, nf=2, lead=()):
    m, d = x.shape
    ff = w_down.shape[-2]
    tf = ff // nf
    tm = _row_tile(m, tm)
    row = lambda i, f: (i, 0)
    fixed = lambda i, f: (0, 0)
    return pl.pallas_call(
        _ffn_kernel,
        grid=(m // tm, nf),
        in_specs=[pl.BlockSpec((tm, d), row), pl.BlockSpec((tm, d), row),
                  _wspec(lead, (d, tf), lambda i, f: (0, f)),
                  _wspec(lead, (d, tf), lambda i, f: (0, nf + f)),
                  _wspec(lead, (tf, d), lambda i, f: (f, 0)),
                  pl.BlockSpec((1, d), fixed), pl.BlockSpec((1, d), fixed)],
        out_specs=[pl.BlockSpec((tm, d), row), pl.BlockSpec((tm, d), row)],
        out_shape=[jax.ShapeDtypeStruct((m, d), F32), jax.ShapeDtypeStruct((m, d), BF16)],
        scratch_shapes=[pltpu.VMEM((tm, d), F32)],
        compiler_params=_params("parallel", "arbitrary"),
        name="ffn_res_ln",
    )(x, h, w_gu, w_gu, w_down, g, b)


def _router_kernel(x_ref, w_ref, dg_ref, meta_ref, cnt_ref, *, n_experts):
    logits = jnp.dot(x_ref[...], w_ref[...], preferred_element_type=F32, precision=lax.Precision.HIGHEST)
    lane = lax.broadcasted_iota(jnp.int32, logits.shape, 1)
    lg = jnp.where(lane < n_experts, logits, -jnp.inf)
    v1 = jnp.max(lg, axis=1, keepdims=True)
    i1 = jnp.min(jnp.where(lg == v1, lane, LANES), axis=1, keepdims=True)
    lg2 = jnp.where(lane == i1, -jnp.inf, lg)
    v2 = jnp.max(lg2, axis=1, keepdims=True)
    i2 = jnp.min(jnp.where(lg2 == v2, lane, LANES), axis=1, keepdims=True)
    e = jnp.exp(v2 - v1)
    g1 = 1.0 / (1.0 + e)
    g2 = e / (1.0 + e)
    pick1, pick2 = lane == i1, lane == i2
    dg_ref[...] = jnp.where(pick1, g1, 0.0) + jnp.where(pick2, g2, 0.0)

    rows = logits.shape[0]
    onehot = jnp.logical_or(pick1, pick2).astype(BF16)
    r = lax.broadcasted_iota(jnp.int32, (rows, rows), 0)
    c = lax.broadcasted_iota(jnp.int32, (rows, rows), 1)
    rank = _dot((c < r).astype(BF16), onehot)
    cnt = jnp.sum(onehot.astype(F32), axis=0, keepdims=True)
    padded = jnp.floor((cnt + (BF16_ROWS - 1)) * (1.0 / BF16_ROWS)) * BF16_ROWS
    er = lax.broadcasted_iota(jnp.int32, (LANES, LANES), 0)
    ec = lax.broadcasted_iota(jnp.int32, (LANES, LANES), 1)
    padded_rows = jnp.broadcast_to(padded, (BF16_ROWS, LANES)).astype(BF16)
    start = _dot(padded_rows, (er < ec).astype(BF16))[0:1, :]
    sorted_row = start + rank
    row1 = jnp.sum(jnp.where(pick1, sorted_row, 0.0), axis=1, keepdims=True)
    row2 = jnp.sum(jnp.where(pick2, sorted_row, 0.0), axis=1, keepdims=True)
    meta_ref[...] = (jnp.where(lane == 0, row1, 0.0) + jnp.where(lane == 1, row2, 0.0)
                     + jnp.where(lane == 2, g1, 0.0) + jnp.where(lane == 3, g2, 0.0))
    cnt_ref[0] = jnp.broadcast_to(cnt, (SUBLANES, LANES))


def router(h, w_router, tm=MOE_BLOCK):
    m, d = h.shape
    e = w_router.shape[1]
    w = jnp.pad(w_router, ((0, 0), (0, LANES - e)))
    tm = _row_tile(m, tm)
    row = lambda i: (i, 0)
    return pl.pallas_call(
        functools.partial(_router_kernel, n_experts=e),
        grid=(m // tm,),
        in_specs=[pl.BlockSpec((tm, d), row), pl.BlockSpec((d, LANES), lambda i: (0, 0))],
        out_specs=[pl.BlockSpec((tm, LANES), row), pl.BlockSpec((tm, LANES), row),
                   pl.BlockSpec((1, SUBLANES, LANES), lambda i: (i, 0, 0))],
        out_shape=[jax.ShapeDtypeStruct((m, LANES), F32), jax.ShapeDtypeStruct((m, LANES), F32),
                   jax.ShapeDtypeStruct((m // tm, SUBLANES, LANES), F32)],
        compiler_params=_params("parallel"),
        name="router",
    )(h, w)


def _moe_dense_kernel(x_ref, h_ref, dg_ref, wg_ref, wu_ref, wd_ref, g_ref, b_ref, of_ref, ob_ref, acc_ref):
    e = pl.program_id(1)
    f = pl.program_id(2)
    dg = dg_ref[...]
    lane = lax.broadcasted_iota(jnp.int32, dg.shape, 1)
    w_e = jnp.sum(jnp.where(lane == e, dg, 0.0), axis=1, keepdims=True)
    part = w_e * _swiglu_part(x_ref[...], wg_ref, wu_ref, wd_ref)
    first = jnp.logical_and(e == 0, f == 0)

    @pl.when(first)
    def _():
        acc_ref[...] = part

    @pl.when(jnp.logical_not(first))
    def _():
        acc_ref[...] += part

    @pl.when(jnp.logical_and(e == pl.num_programs(1) - 1, f == pl.num_programs(2) - 1))
    def _():
        out = _layer_norm(ALPHA * h_ref[...] + acc_ref[...], g_ref[...], b_ref[...])
        of_ref[...] = out
        ob_ref[...] = out.astype(BF16)


def moe_dense_res_ln(x, h, dg, w_gu, w_down, g, b, tm=512, nf=2, lead=()):
    m, d = x.shape
    n_e, ff, _ = w_down.shape[-3:]
    tf = ff // nf
    tm = _row_tile(m, tm)
    row = lambda i, e, f: (i, 0)
    fixed = lambda i, e, f: (0, 0)
    return pl.pallas_call(
        _moe_dense_kernel,
        grid=(m // tm, n_e, nf),
        in_specs=[pl.BlockSpec((tm, d), row), pl.BlockSpec((tm, d), row), pl.BlockSpec((tm, LANES), row),
                  _wspec(lead, (None, d, tf), lambda i, e, f: (e, 0, f)),
                  _wspec(lead, (None, d, tf), lambda i, e, f: (e, 0, nf + f)),
                  _wspec(lead, (None, tf, d), lambda i, e, f: (e, f, 0)),
                  pl.BlockSpec((1, d), fixed), pl.BlockSpec((1, d), fixed)],
        out_specs=[pl.BlockSpec((tm, d), row), pl.BlockSpec((tm, d), row)],
        out_shape=[jax.ShapeDtypeStruct((m, d), F32), jax.ShapeDtypeStruct((m, d), BF16)],
        scratch_shapes=[pltpu.VMEM((tm, d), F32)],
        compiler_params=_params("parallel", "arbitrary", "arbitrary"),
        name="moe_dense_res_ln",
    )(x, h, dg, w_gu, w_gu, w_down, g, b)


def _moe_plan(cnt, n_chunks_max, n_tiles):
    nb, n_e = cnt.shape
    cpt = MOE_BLOCK // BF16_ROWS
    chunks = (cnt + BF16_ROWS - 1) // BF16_ROWS
    seg_local = jnp.cumsum(chunks, axis=1) - chunks
    n_chunks = jnp.sum(chunks, axis=1)
    tiles_e = (jnp.sum(chunks, axis=0) + cpt - 1) // cpt
    tile_end = jnp.cumsum(tiles_e)
    seg_global = ((tile_end - tiles_e) * cpt)[None, :] + jnp.cumsum(chunks, axis=0) - chunks
    c = jnp.arange(n_chunks_max, dtype=jnp.int32)
    e_of_c = jnp.sum(c[None, :, None] >= (seg_local + chunks)[:, None, :], axis=2)
    e_of_c = jnp.minimum(e_of_c, n_e - 1).astype(jnp.int32)
    dst = (jnp.take_along_axis(seg_global, e_of_c, axis=1) + c[None, :]
           - jnp.take_along_axis(seg_local, e_of_c, axis=1))
    tile_expert = jnp.sum(jnp.arange(n_tiles, dtype=jnp.int32)[:, None] >= tile_end[None, :], axis=1)
    tile_expert = jnp.minimum(tile_expert, n_e - 1).astype(jnp.int32)
    return (dst.reshape(-1).astype(jnp.int32), n_chunks.astype(jnp.int32), tile_expert,
            tile_end[-1:].astype(jnp.int32))


def _chunk_copy(src_ref, src_chunk, dst_ref, dst_chunk, sem):
    src = src_ref.at[pl.ds(pl.multiple_of(src_chunk * BF16_ROWS, BF16_ROWS), BF16_ROWS), :]
    dst = dst_ref.at[pl.ds(pl.multiple_of(dst_chunk * BF16_ROWS, BF16_ROWS), BF16_ROWS), :]
    return pltpu.make_async_copy(src, dst, sem)


def _dispatch_kernel(dst_ref, nch_ref, x_ref, rows_ref, buf_in_ref, xs_ref, sorted_ref, sem, *, n_chunks_max):
    del buf_in_ref
    b = pl.program_id(0)
    rows = rows_ref[0]
    n_sorted = sorted_ref.shape[0]
    r = lax.broadcasted_iota(jnp.int32, (n_sorted, rows.shape[1]), 0).astype(F32)
    sel = jnp.logical_or(r == rows[0:1, :], r == rows[1:2, :]).astype(BF16)
    sorted_ref[...] = _dot(sel, x_ref[...]).astype(BF16)
    n = nch_ref[b]

    def copy(c):
        return _chunk_copy(sorted_ref, c, xs_ref, dst_ref[b * n_chunks_max + c], sem)

    def start(c, carry):
        copy(c).start()
        return carry

    def wait(c, carry):
        copy(c).wait()
        return carry

    lax.fori_loop(0, n, start, 0)
    lax.fori_loop(0, n, wait, 0)


def _moe_ffn_kernel(te_ref, nt_ref, x_ref, wg_ref, wu_ref, wd_ref, o_ref, acc_ref):
    del te_ref
    i = pl.program_id(0)
    f = pl.program_id(1)

    @pl.when(i < nt_ref[0])
    def _():
        part = _swiglu_part(x_ref[...], wg_ref, wu_ref, wd_ref)

        @pl.when(f == 0)
        def _():
            acc_ref[...] = part

        @pl.when(f > 0)
        def _():
            acc_ref[...] += part

        @pl.when(f == pl.num_programs(1) - 1)
        def _():
            o_ref[...] = acc_ref[...].astype(o_ref.dtype)

    @pl.when(i >= nt_ref[0])
    def _():
        o_ref[...] = jnp.zeros(o_ref.shape, o_ref.dtype)


def _combine_kernel(dst_ref, nch_ref, ys_ref, meta_ref, h_ref, g_ref, b_ref, of_ref, ob_ref, sorted_ref, sem,
                    *, n_chunks_max):
    blk = pl.program_id(0)
    n = nch_ref[blk]

    def copy(c):
        return _chunk_copy(ys_ref, dst_ref[blk * n_chunks_max + c], sorted_ref, c, sem)

    def start(c, carry):
        copy(c).start()
        return carry

    def clear(c, carry):
        sorted_ref[pl.ds(pl.multiple_of(c * BF16_ROWS, BF16_ROWS), BF16_ROWS), :] = jnp.zeros(
            (BF16_ROWS, sorted_ref.shape[1]), sorted_ref.dtype)
        return carry

    def wait(c, carry):
        copy(c).wait()
        return carry

    lax.fori_loop(0, n, start, 0)
    lax.fori_loop(n, n_chunks_max, clear, 0)
    lax.fori_loop(0, n, wait, 0)
    meta = meta_ref[...]
    col = lax.broadcasted_iota(jnp.int32, (meta.shape[0], sorted_ref.shape[0]), 1).astype(F32)
    ys = sorted_ref[...]
    y = (meta[:, 2:3] * _dot((col == meta[:, 0:1]).astype(BF16), ys)
         + meta[:, 3:4] * _dot((col == meta[:, 1:2]).astype(BF16), ys))
    out = _layer_norm(ALPHA * h_ref[...] + y, g_ref[...], b_ref[...])
    of_ref[...] = out
    ob_ref[...] = out.astype(BF16)


def moe_sorted_res_ln(x, h, meta, cnt, w_gu, w_down, g, b, nf=2, lead=()):
    m, d = x.shape
    n_e, ff, _ = w_down.shape[-3:]
    tf = ff // nf
    nb = m // MOE_BLOCK
    n_chunks_max = 2 * MOE_BLOCK // BF16_ROWS + n_e
    n_sorted = n_chunks_max * BF16_ROWS
    cpt = MOE_BLOCK // BF16_ROWS
    n_tiles = -(-nb * n_chunks_max // cpt) + n_e
    dst, n_chunks, tile_expert, n_used = _moe_plan(cnt[:, 0, :n_e].astype(jnp.int32), n_chunks_max, n_tiles)
    rows_t = jnp.swapaxes(meta[:, :SUBLANES].reshape(nb, MOE_BLOCK, SUBLANES), 1, 2)

    any_space = pl.BlockSpec(memory_space=pl.ANY)
    xs = pl.pallas_call(
        functools.partial(_dispatch_kernel, n_chunks_max=n_chunks_max),
        grid_spec=pltpu.PrefetchScalarGridSpec(
            num_scalar_prefetch=2,
            grid=(nb,),
            in_specs=[pl.BlockSpec((MOE_BLOCK, d), lambda i, *_: (i, 0)),
                      pl.BlockSpec((1, SUBLANES, MOE_BLOCK), lambda i, *_: (i, 0, 0)),
                      any_space],
            out_specs=any_space,
            scratch_shapes=[pltpu.VMEM((n_sorted, d), BF16), pltpu.SemaphoreType.DMA(())],
        ),
        out_shape=jax.ShapeDtypeStruct((n_tiles * MOE_BLOCK, d), BF16),
        input_output_aliases={4: 0},
        compiler_params=_params("arbitrary"),
        name="moe_dispatch",
    )(dst, n_chunks, x, rows_t, jnp.zeros((n_tiles * MOE_BLOCK, d), BF16))

    def tile(i, f, te, nt):
        return jnp.maximum(jnp.minimum(i, nt[0] - 1), 0)

    ys = pl.pallas_call(
        _moe_ffn_kernel,
        grid_spec=pltpu.PrefetchScalarGridSpec(
            num_scalar_prefetch=2,
            grid=(n_tiles, nf),
            in_specs=[pl.BlockSpec((MOE_BLOCK, d), lambda i, f, te, nt: (tile(i, f, te, nt), 0)),
                      _wspec(lead, (None, d, tf), lambda i, f, te, nt: (te[tile(i, f, te, nt)], 0, f)),
                      _wspec(lead, (None, d, tf), lambda i, f, te, nt: (te[tile(i, f, te, nt)], 0, nf + f)),
                      _wspec(lead, (None, tf, d), lambda i, f, te, nt: (te[tile(i, f, te, nt)], f, 0))],
            out_specs=pl.BlockSpec((MOE_BLOCK, d), lambda i, f, te, nt: (i, 0)),
            scratch_shapes=[pltpu.VMEM((MOE_BLOCK, d), F32)],
        ),
        out_shape=jax.ShapeDtypeStruct((n_tiles * MOE_BLOCK, d), BF16),
        compiler_params=_params("arbitrary", "arbitrary"),
        name="moe_expert_ffn",
    )(tile_expert, n_used, xs, w_gu, w_gu, w_down)

    row = lambda i, *_: (i, 0)
    fixed = lambda i, *_: (0, 0)
    return pl.pallas_call(
        functools.partial(_combine_kernel, n_chunks_max=n_chunks_max),
        grid_spec=pltpu.PrefetchScalarGridSpec(
            num_scalar_prefetch=2,
            grid=(nb,),
            in_specs=[any_space, pl.BlockSpec((MOE_BLOCK, LANES), row), pl.BlockSpec((MOE_BLOCK, d), row),
                      pl.BlockSpec((1, d), fixed), pl.BlockSpec((1, d), fixed)],
            out_specs=[pl.BlockSpec((MOE_BLOCK, d), row), pl.BlockSpec((MOE_BLOCK, d), row)],
            scratch_shapes=[pltpu.VMEM((n_sorted, d), BF16), pltpu.SemaphoreType.DMA(())],
        ),
        out_shape=[jax.ShapeDtypeStruct((m, d), F32), jax.ShapeDtypeStruct((m, d), BF16)],
        compiler_params=_params("arbitrary"),
        name="moe_combine_res_ln",
    )(dst, n_chunks, ys, meta, h, g, b)


def _log_sigmoid(x):
    return jnp.minimum(x, 0.0) - jnp.log(1.0 + jnp.exp(-jnp.abs(x)))


def _mlstm_kernel(q_ref, k_ref, v_ref, o_ref, gc_ref, gr_ref, bc_ref, br_ref, gh_ref, c0_ref, n0_ref, m0_ref,
                  hg_ref, c_ref, n_ref, m_ref, *, chunk, t_valid, nh, dh):
    ci = pl.program_id(1)

    @pl.when(ci == 0)
    def _():
        c_ref[...] = c0_ref[...]
        n_ref[...] = n0_ref[...]
        m_ref[...] = m0_ref[...]

    gc = gc_ref[0] + bc_ref[...]
    gr = gr_ref[0] + br_ref[...]
    li_c, lf_c = gc[:, :nh], _log_sigmoid(gc[:, nh:])
    li_r, lf_r = gr[:nh, :], _log_sigmoid(gr[nh:, :])
    if t_valid < chunk:
        tok_c = lax.broadcasted_iota(jnp.int32, li_c.shape, 0) < t_valid
        tok_r = lax.broadcasted_iota(jnp.int32, li_r.shape, 1) < t_valid
        li_c, lf_c = jnp.where(tok_c, li_c, NEG), jnp.where(tok_c, lf_c, 0.0)
        li_r, lf_r = jnp.where(tok_r, li_r, NEG), jnp.where(tok_r, lf_r, 0.0)
    row = lax.broadcasted_iota(jnp.int32, (chunk, chunk), 0)
    col = lax.broadcasted_iota(jnp.int32, (chunk, chunk), 1)
    causal = col <= row
    hi = lax.Precision.HIGHEST
    b_c = jnp.dot(causal.astype(F32), lf_c, preferred_element_type=F32, precision=hi)
    b_r = jnp.dot(lf_r, (row <= col).astype(F32), preferred_element_type=F32, precision=hi)

    mm = q_ref.dtype
    for h in range(nh):
        hs = slice(h * dh, (h + 1) * dh)
        bcol, icol = b_c[:, h:h + 1], li_c[:, h:h + 1]
        brow, irow = b_r[h:h + 1, :], li_r[h:h + 1, :]
        m_prev = m_ref[0, h, :, 0:1]
        c_prev = c_ref[0, h]
        n_prev = n_ref[0, h]
        q = q_ref[:, hs]
        kf = k_ref[:, hs].astype(F32) * (dh ** -0.5)
        k = kf.astype(mm)
        v = v_ref[:, hs]

        dmat = jnp.where(causal, bcol - brow + irow, NEG)
        inter = bcol + m_prev
        mt = jnp.maximum(inter, jnp.max(dmat, axis=1, keepdims=True))
        w_inter = jnp.exp(inter - mt)
        s = _dot_nt(q, k) * jnp.exp(dmat - mt)
        num = w_inter * _dot(q, c_prev.astype(mm)) + _dot(s.astype(mm), v)
        den = (w_inter * jnp.sum(q.astype(F32) * n_prev, axis=1, keepdims=True)
               + jnp.sum(s, axis=1, keepdims=True))
        hh = num / jnp.maximum(jnp.abs(den), jnp.exp(-mt))
        mu = jnp.mean(hh, axis=1, keepdims=True)
        hc = hh - mu
        var = jnp.mean(hc * hc, axis=1, keepdims=True)
        hn = hc * lax.rsqrt(var + LN_EPS) * gh_ref[h:h + 1, :]
        hg_ref[:, hs] = (hn * _sigmoid(o_ref[:, hs])).astype(hg_ref.dtype)

        b_last = bcol[chunk - 1:chunk, :]
        gdec = b_last - bcol + icol
        m_new = jnp.maximum(b_last + m_prev, jnp.max(gdec, axis=0, keepdims=True))
        decay = jnp.exp(b_last + m_prev - m_new)
        kw = kf * jnp.exp(gdec - m_new)
        c_ref[0, h] = decay * c_prev + _dot_tn(kw.astype(mm), v)
        n_ref[0, h] = decay * n_prev + jnp.sum(kw, axis=0, keepdims=True)
        m_ref[0, h] = jnp.broadcast_to(m_new, (1, LANES))


def mlstm(qkv, o, gates, b_gate, g_head, c0, n0, m0, t_valid, chunk):
    bsz, t_len, d3 = qkv.shape
    d = d3 // 3
    nh = g_head.shape[0]
    dh = d // nh
    nc = t_len // chunk
    gates_t = jnp.swapaxes(gates, 1, 2)
    seq = lambda b, c: (b, 0, 0, 0)
    fixed = lambda b, c: (0, 0)
    kern = functools.partial(_mlstm_kernel, chunk=chunk, t_valid=t_valid, nh=nh, dh=dh)
    hg, c, n, m = pl.pallas_call(
        kern,
        grid=(bsz, nc),
        in_specs=[pl.BlockSpec((None, chunk, d), lambda b, c: (b, c, 0)),
                  pl.BlockSpec((None, chunk, d), lambda b, c: (b, c, 1)),
                  pl.BlockSpec((None, chunk, d), lambda b, c: (b, c, 2)),
                  pl.BlockSpec((None, chunk, d), lambda b, c: (b, c, 0)),
                  pl.BlockSpec((1, chunk, 2 * nh), lambda b, c: (b, c, 0)),
                  pl.BlockSpec((1, 2 * nh, chunk), lambda b, c: (b, 0, c)),
                  pl.BlockSpec((1, 2 * nh), fixed), pl.BlockSpec((2 * nh, 1), fixed),
                  pl.BlockSpec((nh, dh), fixed),
                  pl.BlockSpec((1, nh, dh, dh), seq), pl.BlockSpec((1, nh, 1, dh), seq),
                  pl.BlockSpec((1, nh, 1, LANES), seq)],
        out_specs=[pl.BlockSpec((None, chunk, d), lambda b, c: (b, c, 0)),
                   pl.BlockSpec((1, nh, dh, dh), seq), pl.BlockSpec((1, nh, 1, dh), seq),
                   pl.BlockSpec((1, nh, 1, LANES), seq)],
        out_shape=[jax.ShapeDtypeStruct((bsz, t_len, d), qkv.dtype),
                   jax.ShapeDtypeStruct((bsz, nh, dh, dh), F32),
                   jax.ShapeDtypeStruct((bsz, nh, 1, dh), F32),
                   jax.ShapeDtypeStruct((bsz, nh, 1, LANES), F32)],
        compiler_params=_params("parallel", "arbitrary"),
        name="mlstm",
    )(qkv, qkv, qkv, o, gates, gates_t, b_gate.reshape(1, 2 * nh), b_gate.reshape(2 * nh, 1), g_head,
      c0, n0.reshape(bsz, nh, 1, dh), jnp.broadcast_to(m0[:, :, None, None], (bsz, nh, 1, LANES)))
    return hg, c, n.reshape(bsz, nh, dh), m[:, :, 0, 0]


def _lambda_value(lam_ref, lam_init):
    lp = lam_ref[...]
    a = jnp.sum(lp[0:1, :] * lp[1:2, :], axis=1, keepdims=True)
    b = jnp.sum(lp[2:3, :] * lp[3:4, :], axis=1, keepdims=True)
    return jnp.exp(a) - jnp.exp(b) + lam_init


def _split_queries(q, dh):
    lane = lax.broadcasted_iota(jnp.int32, q.shape, 1)
    zero = jnp.zeros_like(q)
    return jnp.concatenate([jnp.where(lane < dh, q, zero), jnp.where(lane >= dh, q, zero)], axis=0)


def _flash_kernel(q_ref, k_ref, vt_ref, lam_ref, g_ref, o_ref, qs_ref, s_ref, m_ref, l_ref, acc_ref,
                  *, blk, dh, lam_init):
    qi = pl.program_id(2)
    qs_ref[...] = _split_queries(q_ref[...], dh)
    m_ref[...] = jnp.full(m_ref.shape, NEG, F32)
    l_ref[...] = jnp.zeros(l_ref.shape, F32)
    acc_ref[...] = jnp.zeros(acc_ref.shape, F32)
    sub = 2 * blk // ATTN_QSPLIT

    def scores(ki, c):
        k = k_ref[pl.ds(pl.multiple_of(ki * blk, blk), blk), :]
        return _dot_nt(k, qs_ref[c * sub:(c + 1) * sub, :])

    def update(ki, c, s, diagonal):
        cs = slice(c * sub, (c + 1) * sub)
        vt = vt_ref[:, pl.ds(pl.multiple_of(ki * blk, blk), blk)]
        if diagonal:
            key = lax.broadcasted_iota(jnp.int32, s.shape, 0)
            qry = (lax.broadcasted_iota(jnp.int32, s.shape, 1) + c * sub) % blk
            s = jnp.where(key <= qry, s, NEG)
        m_prev = m_ref[:, cs]
        m_new = jnp.maximum(m_prev, jnp.max(s, axis=0, keepdims=True))
        alpha = jnp.exp2(m_prev - m_new)
        p = jnp.exp2(s - m_new)
        l_ref[:, cs] = alpha * l_ref[:, cs] + jnp.sum(p, axis=0, keepdims=True)
        acc_ref[:, cs] = alpha * acc_ref[:, cs] + _dot(vt, p.astype(BF16))
        m_ref[:, cs] = m_new

    def block(ki, diagonal):
        pending = [s_ref[i] for i in range(ATTN_LOOKAHEAD)]
        for c in range(ATTN_QSPLIT):
            nxt = c + ATTN_LOOKAHEAD
            if nxt < ATTN_QSPLIT:
                pending.append(scores(ki, nxt))
            elif not diagonal:
                pending.append(scores(ki + 1, nxt - ATTN_QSPLIT))
            update(ki, c, pending.pop(0), diagonal)
        for i, s in enumerate(pending):
            s_ref[i] = s

    def body(ki, carry):
        block(ki, False)
        return carry

    for i in range(ATTN_LOOKAHEAD):
        s_ref[i] = scores(0, i)
    lax.fori_loop(0, qi, body, 0)
    block(qi, True)
    lam = _lambda_value(lam_ref, lam_init)
    acc = acc_ref[...]
    l = l_ref[...]
    o = acc[:, :blk] / l[:, :blk] - lam * (acc[:, blk:] / l[:, blk:])
    ms = jnp.mean(o * o, axis=0, keepdims=True)
    o = o * lax.rsqrt(ms + LN_EPS) * (g_ref[...] * (1.0 - lam_init))
    o_ref[...] = o.T.astype(o_ref.dtype)


def flash_diff_attention(q, k, vt, lam_p, g_subln, lam_init, n_heads, blk=ATTN_BLOCK):
    bsz, t_len, d = q.shape
    hw = d // n_heads
    dh = hw // 2
    blk = min(blk, t_len)
    fixed = lambda b, h, i: (0, 0)
    return pl.pallas_call(
        functools.partial(_flash_kernel, blk=blk, dh=dh, lam_init=lam_init),
        grid=(bsz, n_heads, t_len // blk),
        in_specs=[pl.BlockSpec((None, blk, hw), lambda b, h, i: (b, i, h)),
                  pl.BlockSpec((None, t_len, hw), lambda b, h, i: (b, 0, h)),
                  pl.BlockSpec((None, hw, t_len), lambda b, h, i: (b, h, 0)),
                  pl.BlockSpec(lam_p.shape, fixed), pl.BlockSpec((hw, 1), fixed)],
        out_specs=pl.BlockSpec((None, blk, hw), lambda b, h, i: (b, i, h)),
        out_shape=jax.ShapeDtypeStruct((bsz, t_len, d), BF16),
        scratch_shapes=[pltpu.VMEM((2 * blk, hw), BF16), pltpu.VMEM((ATTN_LOOKAHEAD, blk, 2 * blk // ATTN_QSPLIT), F32),
                        pltpu.VMEM((1, 2 * blk), F32), pltpu.VMEM((1, 2 * blk), F32),
                        pltpu.VMEM((hw, 2 * blk), F32)],
        compiler_params=_params("parallel", "parallel", "arbitrary"),
        name="flash_diff_attention",
    )(q, k, vt, lam_p, g_subln.reshape(hw, 1))


def _decode_kernel(pt_ref, q_ref, *refs, nh, dh, lam_init, pps):
    del pt_ref
    k_refs, v_refs = refs[:pps], refs[pps:2 * pps]
    kn_ref, vn_ref, lam_ref, g_ref, o_ref, qs_ref, m_ref, l_ref, acc_ref = refs[2 * pps:]
    j = pl.program_id(1)

    @pl.when(j == 0)
    def _():
        qs = _split_queries(q_ref[0], dh) * (dh ** -0.5)
        qs_ref[...] = qs.astype(BF16)
        kn = jnp.concatenate([kn_ref[0, 0], kn_ref[0, 0]], axis=0)
        m_ref[...] = jnp.sum(qs * kn, axis=1, keepdims=True)
        l_ref[...] = jnp.ones(l_ref.shape, F32)
        acc_ref[...] = jnp.concatenate([vn_ref[0, 0], vn_ref[0, 0]], axis=0)

    page = k_refs[0].shape[1]
    qs = qs_ref[...]
    s = jnp.concatenate([_dot_nt(qs, k_ref[0].reshape(page * nh, 2 * dh).astype(BF16)) for k_ref in k_refs], axis=1)
    row = lax.broadcasted_iota(jnp.int32, s.shape, 0) % nh
    col = lax.broadcasted_iota(jnp.int32, s.shape, 1) % nh
    s = jnp.where(row == col, s, NEG)
    m_prev = m_ref[...]
    m_new = jnp.maximum(m_prev, jnp.max(s, axis=1, keepdims=True))
    alpha = jnp.exp(m_prev - m_new)
    p = jnp.exp(s - m_new).astype(BF16)
    l_ref[...] = alpha * l_ref[...] + jnp.sum(p.astype(F32), axis=1, keepdims=True)
    pv = _dot(p[:, :page * nh], v_refs[0][0].reshape(page * nh, 2 * dh).astype(BF16))
    for i in range(1, pps):
        pv += _dot(p[:, i * page * nh:(i + 1) * page * nh], v_refs[i][0].reshape(page * nh, 2 * dh).astype(BF16))
    acc_ref[...] = alpha * acc_ref[...] + pv
    m_ref[...] = m_new

    @pl.when(j == pl.num_programs(1) - 1)
    def _():
        lam = _lambda_value(lam_ref, lam_init)
        acc = acc_ref[...]
        l = l_ref[...]
        o = acc[:nh] / l[:nh] - lam * (acc[nh:] / l[nh:])
        ms = jnp.mean(o * o, axis=1, keepdims=True)
        o_ref[0] = o * lax.rsqrt(ms + LN_EPS) * g_ref[...] * (1.0 - lam_init)


def decode_diff_attention(q, cache_k, cache_v, page_table, k_new, v_new, lam_p, g_subln, lam_init):
    n_seq, nh, hw = q.shape
    dh = hw // 2
    n_pages = page_table.shape[1]
    page = cache_k.shape[1]
    pps = math.gcd(n_pages, DECODE_PAGES)
    seq = lambda s, j, pt: (s, 0, 0)
    new = lambda s, j, pt: (s, 0, 0, 0)
    fixed = lambda s, j, pt: (0, 0)

    def paged(i):
        return pl.BlockSpec((1, page, nh, hw), lambda s, j, pt: (pt[s, j * pps + i], 0, 0, 0))

    grid_spec = pltpu.PrefetchScalarGridSpec(
        num_scalar_prefetch=1,
        grid=(n_seq, n_pages // pps),
        in_specs=([pl.BlockSpec((1, nh, hw), seq)] + [paged(i) for i in range(pps)] * 2
                  + [pl.BlockSpec((1, 1, nh, hw), new), pl.BlockSpec((1, 1, nh, hw), new),
                     pl.BlockSpec(lam_p.shape, fixed), pl.BlockSpec((1, hw), fixed)]),
        out_specs=pl.BlockSpec((1, nh, hw), seq),
        scratch_shapes=[pltpu.VMEM((2 * nh, hw), BF16), pltpu.VMEM((2 * nh, 1), F32),
                        pltpu.VMEM((2 * nh, 1), F32), pltpu.VMEM((2 * nh, hw), F32)],
    )
    return pl.pallas_call(
        functools.partial(_decode_kernel, nh=nh, dh=dh, lam_init=lam_init, pps=pps),
        grid_spec=grid_spec,
        out_shape=jax.ShapeDtypeStruct((n_seq, nh, hw), F32),
        compiler_params=_params("parallel", "arbitrary"),
        name="decode_diff_attention",
    )(page_table, q, *([cache_k] * pps), *([cache_v] * pps), k_new, v_new, lam_p, g_subln.reshape(1, hw))


def _trunk(x, state, past, wts):
    bsz, t_len, d = x.shape
    m_rows = bsz * t_len
    n_a = wts["w_in"].shape[0]
    hw = wts["g_subln"].shape[-1]
    n_heads = d // hw
    act = wts["w_in"].dtype
    pick = (lambda h32, hbf: hbf) if act == BF16 else (lambda h32, hbf: h32)
    h32 = x.reshape(m_rows, d)
    ha = h32.astype(act)
    cs, ns, ms = [], [], []
    k32 = v32 = kbf = vt = None
    for i in range(DEPTH):
        ln_g, ln_b = wts["ln_g"][i], wts["ln_b"][i]
        if i < n_a:
            (qkv,) = matmul(ha, wts["w_in"], [act], lead=(i,), n=3 * d)
            (og,) = matmul(ha, wts["w_in"], [F32], lead=(i,), col0=3, n=d)
            (gates,) = matmul(ha, wts["w_gates"], [F32], lead=(i,))
            chunk = MLSTM_CHUNK if t_len % MLSTM_CHUNK == 0 else BF16_ROWS
            t_pad = -(-t_len // chunk) * chunk
            gates = gates[:, :2 * A_HEADS].reshape(bsz, t_len, 2 * A_HEADS)
            og = og.reshape(bsz, t_len, d)
            qkv = qkv.reshape(bsz, t_len, 3 * d)
            if t_pad != t_len:
                pad = ((0, 0), (0, t_pad - t_len), (0, 0))
                qkv, og, gates = jnp.pad(qkv, pad), jnp.pad(og, pad), jnp.pad(gates, pad)
            if state is None:
                dh = d // A_HEADS
                c0 = jnp.zeros((bsz, A_HEADS, dh, dh), F32)
                n0 = jnp.zeros((bsz, A_HEADS, dh), F32)
                m0 = jnp.zeros((bsz, A_HEADS), F32)
            else:
                c0, n0, m0 = state[0][i], state[1][i], state[2][i]
            hg, c, n, m = mlstm(qkv, og, gates, wts["b_gate"][i], wts["g_head"][i], c0, n0, m0, t_len, chunk)
            cs.append(c)
            ns.append(n)
            ms.append(m)
            mix_in = hg[:, :t_len].reshape(m_rows, d)
            w_mix, mix_lead = wts["w_out"], (i,)
        else:
            j = i - n_a
            lam_init = _lambda_init(i)
            if j == 0:
                if past is None:
                    k32, kbf = matmul(ha, wts["w_kv"], [F32, BF16], n=d)
                    (v32,) = matmul(ha, wts["w_kv"], [F32], col0=1, n=d)
                    vt = matmul_t(ha.reshape(bsz, t_len, d), wts["w_v_t"])
                else:
                    (k32,) = matmul(ha, wts["w_kv"], [F32], n=d)
                    (v32,) = matmul(ha, wts["w_kv"], [F32], col0=1, n=d)
            if past is None:
                (q,) = matmul(ha, wts["w_q"], [BF16], lead=(j,), scale=math.log2(math.e) * (hw // 2) ** -0.5)
                o = flash_diff_attention(q.reshape(bsz, t_len, d), kbf.reshape(bsz, t_len, d), vt,
                                         wts["lam"][j], wts["g_subln"][j], lam_init, n_heads)
                mix_in = o.reshape(m_rows, d)
            else:
                (q,) = matmul(ha, wts["w_q"], [F32], lead=(j,))
                cache_k, cache_v, page_table = past
                o = decode_diff_attention(q.reshape(bsz, n_heads, hw), cache_k, cache_v, page_table,
                                          k32.reshape(bsz, 1, n_heads, hw), v32.reshape(bsz, 1, n_heads, hw),
                                          wts["lam"][j], wts["g_subln"][j], lam_init)
                mix_in = o.reshape(m_rows, d).astype(act)
            w_mix, mix_lead = wts["w_o"], (j,)
        h32, hbf = matmul_res_ln(mix_in, w_mix, h32, ln_g[0:1], ln_b[0:1], lead=mix_lead)
        ha = pick(h32, hbf)
        if i % 2 == 0:
            h32, hbf = ffn_res_ln(ha, h32, wts["w_ffn_gu"], wts["w_ffn_down"], ln_g[1:2], ln_b[1:2], lead=(i // 2,))
        else:
            dg, meta, cnt = router(h32, wts["w_router"][i // 2])
            w_gu, w_down, lead = wts["w_moe_gu"], wts["w_moe_down"], (i // 2,)
            if m_rows >= MOE_MIN_ROWS and m_rows % MOE_BLOCK == 0 and act == BF16:
                h32, hbf = moe_sorted_res_ln(ha, h32, meta, cnt, w_gu, w_down, ln_g[1:2], ln_b[1:2], lead=lead)
            else:
                h32, hbf = moe_dense_res_ln(ha, h32, dg, w_gu, w_down, ln_g[1:2], ln_b[1:2], lead=lead)
        ha = pick(h32, hbf)
    y = h32.reshape(bsz, t_len, d)
    k_rows = k32.reshape(bsz, t_len, n_heads, hw)
    v_rows = v32.reshape(bsz, t_len, n_heads, hw)
    return y, jnp.stack(cs), jnp.stack(ns), jnp.stack(ms), k_rows, v_rows


def kernel(x_prompt, x_sample, state_c, state_n, state_m, cache_k, cache_v, page_table, ln_g, ln_b, w_in_a, b_gate_a, g_head_a, w_out_a, w_kv, w_q_b, lam_b, g_subln_b, w_o_b, w_ffn_gu, w_ffn_down, w_router, w_moe_gu, w_moe_down):
    d = x_prompt.shape[-1]
    n_gate = w_in_a.shape[-1] - 4 * d
    w_gates = jnp.pad(w_in_a[:, :, 4 * d:], ((0, 0), (0, 0), (0, LANES - n_gate)))

    def weights(cast):
        return {
            "ln_g": ln_g, "ln_b": ln_b, "b_gate": b_gate_a, "g_head": g_head_a, "lam": lam_b,
            "g_subln": g_subln_b, "w_router": w_router,
            "w_in": cast(w_in_a), "w_gates": cast(w_gates), "w_out": cast(w_out_a), "w_kv": cast(w_kv),
            "w_v_t": cast(w_kv[:, d:].T), "w_q": cast(w_q_b), "w_o": cast(w_o_b), "w_ffn_gu": cast(w_ffn_gu),
            "w_ffn_down": cast(w_ffn_down), "w_moe_gu": cast(w_moe_gu), "w_moe_down": cast(w_moe_down),
        }

    y_p, c_p, n_p, m_p, k_p, v_p = _trunk(x_prompt, None, None, weights(lambda a: a.astype(BF16)))
    y_s, c_s, n_s, m_s, k_s, v_s = _trunk(x_sample, (state_c, state_n, state_m),
                                          (cache_k, cache_v, page_table), weights(lambda a: a))
    return (y_p, y_s, c_p, n_p, m_p, k_p, v_p, c_s, n_s, m_s, k_s, v_s)
```

```python
import functools
import math

import jax
import jax.numpy as jnp
from jax import lax
from jax.experimental import pallas as pl
from jax.experimental.pallas import tpu as pltpu

F32 = jnp.float32
BF16 = jnp.bfloat16

DEPTH = 4
ALPHA = (2 * DEPTH) ** 0.25
LN_EPS = 1e-5
NEG = -1e30
LANES = 128
SUBLANES = 8
BF16_ROWS = 16
VMEM_LIMIT = 56 * 1024 * 1024

A_HEADS = 4
MLSTM_CHUNK = 256
ATTN_BLOCK = 512
ATTN_QSPLIT = 2
ATTN_LOOKAHEAD = 2
MOE_BLOCK = 512
MOE_MIN_ROWS = 1024


def _lambda_init(layer_idx):
    return 0.8 - 0.6 * math.exp(-0.3 * layer_idx)


def _params(*sem):
    return pltpu.CompilerParams(dimension_semantics=sem, vmem_limit_bytes=VMEM_LIMIT)


def _contract(a, b, dims):
    precision = lax.Precision.HIGHEST if a.dtype == F32 and b.dtype == F32 else None
    return lax.dot_general(a, b, (dims, ((), ())), preferred_element_type=F32, precision=precision)


def _dot(a, b):
    return _contract(a, b, ((1,), (0,)))


def _dot_nt(a, b):
    return _contract(a, b, ((1,), (1,)))


def _dot_tn(a, b):
    return _contract(a, b, ((0,), (0,)))


def _layer_norm(y, g, b):
    mu = jnp.mean(y, axis=-1, keepdims=True)
    yc = y - mu
    var = jnp.mean(yc * yc, axis=-1, keepdims=True)
    return yc * lax.rsqrt(var + LN_EPS) * g + b


def _sigmoid(x):
    return 0.5 * jnp.tanh(0.5 * x) + 0.5


def _row_tile(m, want):
    return want if m % want == 0 else m


def _wspec(lead, block, index):
    return pl.BlockSpec((None,) * len(lead) + block, lambda *a: tuple(lead) + index(*a))


def _mm_kernel(x_ref, w_ref, *o_refs, scale):
    acc = _dot(x_ref[...], w_ref[...])
    if scale != 1.0:
        acc = acc * scale
    for o_ref in o_refs:
        o_ref[...] = acc.astype(o_ref.dtype)


def matmul(x, w, out_dtypes, tm=1024, tn=1024, scale=1.0, lead=(), col0=0, n=None):
    m, k = x.shape
    n = w.shape[-1] if n is None else n
    tm = _row_tile(m, tm)
    tn = _row_tile(n, tn)
    return pl.pallas_call(
        functools.partial(_mm_kernel, scale=scale),
        grid=(m // tm, n // tn),
        in_specs=[pl.BlockSpec((tm, k), lambda i, j: (i, 0)),
                  _wspec(lead, (k, tn), lambda i, j: (0, col0 + j))],
        out_specs=[pl.BlockSpec((tm, tn), lambda i, j: (i, j)) for _ in out_dtypes],
        out_shape=[jax.ShapeDtypeStruct((m, n), d) for d in out_dtypes],
        compiler_params=_params("parallel", "parallel"),
        name="matmul",
    )(x, w)


def _mm_t_kernel(x_ref, wt_ref, o_ref):
    o_ref[...] = _dot_nt(wt_ref[...], x_ref[...]).astype(o_ref.dtype)


def matmul_t(x, wt, tm=1024, tn=1024):
    bsz, t_len, k = x.shape
    n = wt.shape[0]
    tm = _row_tile(t_len, tm)
    tn = _row_tile(n, tn)
    return pl.pallas_call(
        _mm_t_kernel,
        grid=(bsz, t_len // tm, n // tn),
        in_specs=[pl.BlockSpec((None, tm, k), lambda b, i, j: (b, i, 0)),
                  pl.BlockSpec((tn, k), lambda b, i, j: (j, 0))],
        out_specs=pl.BlockSpec((None, tn, tm), lambda b, i, j: (b, j, i)),
        out_shape=jax.ShapeDtypeStruct((bsz, n, t_len), BF16),
        compiler_params=_params("parallel", "parallel", "parallel"),
        name="matmul_t",
    )(x, wt)


def _mm_res_ln_kernel(x_ref, w_ref, h_ref, g_ref, b_ref, of_ref, ob_ref):
    y = ALPHA * h_ref[...] + _dot(x_ref[...], w_ref[...])
    out = _layer_norm(y, g_ref[...], b_ref[...])
    of_ref[...] = out
    ob_ref[...] = out.astype(BF16)


def matmul_res_ln(x, w, h, g, b, tm=1024, lead=()):
    m, k = x.shape
    d = w.shape[-1]
    tm = _row_tile(m, tm)
    row = lambda i: (i, 0)
    fixed = lambda i: (0, 0)
    return pl.pallas_call(
        _mm_res_ln_kernel,
        grid=(m // tm,),
        in_specs=[pl.BlockSpec((tm, k), row), _wspec(lead, (k, d), fixed), pl.BlockSpec((tm, d), row),
                  pl.BlockSpec((1, d), fixed), pl.BlockSpec((1, d), fixed)],
        out_specs=[pl.BlockSpec((tm, d), row), pl.BlockSpec((tm, d), row)],
        out_shape=[jax.ShapeDtypeStruct((m, d), F32), jax.ShapeDtypeStruct((m, d), BF16)],
        compiler_params=_params("parallel"),
        name="matmul_res_ln",
    )(x, w, h, g, b)


def _swiglu_part(x, wg_ref, wu_ref, wd_ref):
    gate = _dot(x, wg_ref[...])
    up = _dot(x, wu_ref[...])
    act = (gate * _sigmoid(gate) * up).astype(wd_ref.dtype)
    return _dot(act, wd_ref[...])


def _ffn_kernel(x_ref, h_ref, wg_ref, wu_ref, wd_ref, g_ref, b_ref, of_ref, ob_ref, acc_ref):
    f = pl.program_id(1)
    part = _swiglu_part(x_ref[...], wg_ref, wu_ref, wd_ref)

    @pl.when(f == 0)
    def _():
        acc_ref[...] = part

    @pl.when(f > 0)
    def _():
        acc_ref[...] += part

    @pl.when(f == pl.num_programs(1) - 1)
    def _():
        out = _layer_norm(ALPHA * h_ref[...] + acc_ref[...], g_ref[...], b_ref[...])
        of_ref[...] = out
        ob_ref[...] = out.astype(BF16)


def ffn_res_ln(x, h, w_gu, w_down, g, b, tm=512, nf=2, lead=()):
    m, d = x.shape
    ff = w_down.shape[-2]
    tf = ff // nf
    tm = _row_tile(m, tm)
    row = lambda i, f: (i, 0)
    fixed = lambda i, f: (0, 0)
    return pl.pallas_call(
        _ffn_kernel,
        grid=(m // tm, nf),
        in_specs=[pl.BlockSpec((tm, d), row), pl.BlockSpec((tm, d), row),
                  _wspec(lead, (d, tf), lambda i, f: (0, f)),
                  _wspec(lead, (d, tf), lambda i, f: (0, nf + f)),
                  _wspec(lead, (tf, d), lambda i, f: (f, 0)),
                  pl.BlockSpec((1, d), fixed), pl.BlockSpec((1, d), fixed)],
        out_specs=[pl.BlockSpec((tm, d), row), pl.BlockSpec((tm, d), row)],
        out_shape=[jax.ShapeDtypeStruct((m, d), F32), jax.ShapeDtypeStruct((m, d), BF16)],
        scratch_shapes=[pltpu.VMEM((tm, d), F32)],
        compiler_params=_params("parallel", "arbitrary"),
        name="ffn_res_ln",
    )(x, h, w_gu, w_gu, w_down, g, b)


def _router_kernel(x_ref, w_ref, dg_ref, meta_ref, cnt_ref, *, n_experts):
    logits = jnp.dot(x_ref[...], w_ref[...], preferred_element_type=F32, precision=lax.Precision.HIGHEST)
    lane = lax.broadcasted_iota(jnp.int32, logits.shape, 1)
    lg = jnp.where(lane < n_experts, logits, -jnp.inf)
    v1 = jnp.max(lg, axis=1, keepdims=True)
    i1 = jnp.min(jnp.where(lg == v1, lane, LANES), axis=1, keepdims=True)
    lg2 = jnp.where(lane == i1, -jnp.inf, lg)
    v2 = jnp.max(lg2, axis=1, keepdims=True)
    i2 = jnp.min(jnp.where(lg2 == v2, lane, LANES), axis=1, keepdims=True)
    e = jnp.exp(v2 - v1)
    g1 = 1.0 / (1.0 + e)
    g2 = e / (1.0 + e)
    pick1, pick2 = lane == i1, lane == i2
    dg_ref[...] = jnp.where(pick1, g1, 0.0) + jnp.where(pick2, g2, 0.0)

    rows = logits.shape[0]
    onehot = jnp.logical_or(pick1, pick2).astype(BF16)
    r = lax.broadcasted_iota(jnp.int32, (rows, rows), 0)
    c = lax.broadcasted_iota(jnp.int32, (rows, rows), 1)
    rank = _dot((c < r).astype(BF16), onehot)
    cnt = jnp.sum(onehot.astype(F32), axis=0, keepdims=True)
    padded = jnp.floor((cnt + (BF16_ROWS - 1)) * (1.0 / BF16_ROWS)) * BF16_ROWS
    er = lax.broadcasted_iota(jnp.int32, (LANES, LANES), 0)
    ec = lax.broadcasted_iota(jnp.int32, (LANES, LANES), 1)
    padded_rows = jnp.broadcast_to(padded, (BF16_ROWS, LANES)).astype(BF16)
    start = _dot(padded_rows, (er < ec).astype(BF16))[0:1, :]
    sorted_row = start + rank
    row1 = jnp.sum(jnp.where(pick1, sorted_row, 0.0), axis=1, keepdims=True)
    row2 = jnp.sum(jnp.where(pick2, sorted_row, 0.0), axis=1, keepdims=True)
    meta_ref[...] = (jnp.where(lane == 0, row1, 0.0) + jnp.where(lane == 1, row2, 0.0)
                     + jnp.where(lane == 2, g1, 0.0) + jnp.where(lane == 3, g2, 0.0))
    cnt_ref[0] = jnp.broadcast_to(cnt, (SUBLANES, LANES))


def router(h, w_router, tm=MOE_BLOCK):
    m, d = h.shape
    e = w_router.shape[1]
    w = jnp.pad(w_router, ((0, 0), (0, LANES - e)))
    tm = _row_tile(m, tm)
    row = lambda i: (i, 0)
    return pl.pallas_call(
        functools.partial(_router_kernel, n_experts=e),
        grid=(m // tm,),
        in_specs=[pl.BlockSpec((tm, d), row), pl.BlockSpec((d, LANES), lambda i: (0, 0))],
        out_specs=[pl.BlockSpec((tm, LANES), row), pl.BlockSpec((tm, LANES), row),
                   pl.BlockSpec((1, SUBLANES, LANES), lambda i: (i, 0, 0))],
        out_shape=[jax.ShapeDtypeStruct((m, LANES), F32), jax.ShapeDtypeStruct((m, LANES), F32),
                   jax.ShapeDtypeStruct((m // tm, SUBLANES, LANES), F32)],
        compiler_params=_params("parallel"),
        name="router",
    )(h, w)


def _moe_dense_kernel(x_ref, h_ref, dg_ref, wg_ref, wu_ref, wd_ref, g_ref, b_ref, of_ref, ob_ref, acc_ref):
    e = pl.program_id(1)
    f = pl.program_id(2)
    dg = dg_ref[...]
    lane = lax.broadcasted_iota(jnp.int32, dg.shape, 1)
    w_e = jnp.sum(jnp.where(lane == e, dg, 0.0), axis=1, keepdims=True)
    part = w_e * _swiglu_part(x_ref[...], wg_ref, wu_ref, wd_ref)
    first = jnp.logical_and(e == 0, f == 0)

    @pl.when(first)
    def _():
        acc_ref[...] = part

    @pl.when(jnp.logical_not(first))
    def _():
        acc_ref[...] += part

    @pl.when(jnp.logical_and(e == pl.num_programs(1) - 1, f == pl.num_programs(2) - 1))
    def _():
        out = _layer_norm(ALPHA * h_ref[...] + acc_ref[...], g_ref[...], b_ref[...])
        of_ref[...] = out
        ob_ref[...] = out.astype(BF16)


def moe_dense_res_ln(x, h, dg, w_gu, w_down, g, b, tm=512, nf=2, lead=()):
    m, d = x.shape
    n_e, ff, _ = w_down.shape[-3:]
    tf = ff // nf
    tm = _row_tile(m, tm)
    row = lambda i, e, f: (i, 0)
    fixed = lambda i, e, f: (0, 0)
    return pl.pallas_call(
        _moe_dense_kernel,
        grid=(m // tm, n_e, nf),
        in_specs=[pl.BlockSpec((tm, d), row), pl.BlockSpec((tm, d), row), pl.BlockSpec((tm, LANES), row),
                  _wspec(lead, (None, d, tf), lambda i, e, f: (e, 0, f)),
                  _wspec(lead, (None, d, tf), lambda i, e, f: (e, 0, nf + f)),
                  _wspec(lead, (None, tf, d), lambda i, e, f: (e, f, 0)),
                  pl.BlockSpec((1, d), fixed), pl.BlockSpec((1, d), fixed)],
        out_specs=[pl.BlockSpec((tm, d), row), pl.BlockSpec((tm, d), row)],
        out_shape=[jax.ShapeDtypeStruct((m, d), F32), jax.ShapeDtypeStruct((m, d), BF16)],
        scratch_shapes=[pltpu.VMEM((tm, d), F32)],
        compiler_params=_params("parallel", "arbitrary", "arbitrary"),
        name="moe_dense_res_ln",
    )(x, h, dg, w_gu, w_gu, w_down, g, b)


def _moe_plan(cnt, n_chunks_max, n_tiles):
    nb, n_e = cnt.shape
    cpt = MOE_BLOCK // BF16_ROWS
    chunks = (cnt + BF16_ROWS - 1) // BF16_ROWS
    seg_local = jnp.cumsum(chunks, axis=1) - chunks
    n_chunks = jnp.sum(chunks, axis=1)
    tiles_e = (jnp.sum(chunks, axis=0) + cpt - 1) // cpt
    tile_end = jnp.cumsum(tiles_e)
    seg_global = ((tile_end - tiles_e) * cpt)[None, :] + jnp.cumsum(chunks, axis=0) - chunks
    c = jnp.arange(n_chunks_max, dtype=jnp.int32)
    e_of_c = jnp.sum(c[None, :, None] >= (seg_local + chunks)[:, None, :], axis=2)
    e_of_c = jnp.minimum(e_of_c, n_e - 1).astype(jnp.int32)
    dst = (jnp.take_along_axis(seg_global, e_of_c, axis=1) + c[None, :]
           - jnp.take_along_axis(seg_local, e_of_c, axis=1))
    tile_expert = jnp.sum(jnp.arange(n_tiles, dtype=jnp.int32)[:, None] >= tile_end[None, :], axis=1)
    tile_expert = jnp.minimum(tile_expert, n_e - 1).astype(jnp.int32)
    return (dst.reshape(-1).astype(jnp.int32), n_chunks.astype(jnp.int32), tile_expert,
            tile_end[-1:].astype(jnp.int32))


def _chunk_copy(src_ref, src_chunk, dst_ref, dst_chunk, sem):
    src = src_ref.at[pl.ds(pl.multiple_of(src_chunk * BF16_ROWS, BF16_ROWS), BF16_ROWS), :]
    dst = dst_ref.at[pl.ds(pl.multiple_of(dst_chunk * BF16_ROWS, BF16_ROWS), BF16_ROWS), :]
    return pltpu.make_async_copy(src, dst, sem)


def _dispatch_kernel(dst_ref, nch_ref, x_ref, rows_ref, buf_in_ref, xs_ref, sorted_ref, sem, *, n_chunks_max):
    del buf_in_ref
    b = pl.program_id(0)
    rows = rows_ref[0]
    n_sorted = sorted_ref.shape[0]
    r = lax.broadcasted_iota(jnp.int32, (n_sorted, rows.shape[1]), 0).astype(F32)
    sel = jnp.logical_or(r == rows[0:1, :], r == rows[1:2, :]).astype(BF16)
    sorted_ref[...] = _dot(sel, x_ref[...]).astype(BF16)
    n = nch_ref[b]

    def copy(c):
        return _chunk_copy(sorted_ref, c, xs_ref, dst_ref[b * n_chunks_max + c], sem)

    def start(c, carry):
        copy(c).start()
        return carry

    def wait(c, carry):
        copy(c).wait()
        return carry

    lax.fori_loop(0, n, start, 0)
    lax.fori_loop(0, n, wait, 0)


def _moe_ffn_kernel(te_ref, nt_ref, x_ref, wg_ref, wu_ref, wd_ref, o_ref, acc_ref):
    del te_ref
    i = pl.program_id(0)
    f = pl.program_id(1)

    @pl.when(i < nt_ref[0])
    def _():
        part = _swiglu_part(x_ref[...], wg_ref, wu_ref, wd_ref)

        @pl.when(f == 0)
        def _():
            acc_ref[...] = part

        @pl.when(f > 0)
        def _():
            acc_ref[...] += part

        @pl.when(f == pl.num_programs(1) - 1)
        def _():
            o_ref[...] = acc_ref[...].astype(o_ref.dtype)

    @pl.when(i >= nt_ref[0])
    def _():
        o_ref[...] = jnp.zeros(o_ref.shape, o_ref.dtype)


def _combine_kernel(dst_ref, nch_ref, ys_ref, meta_ref, h_ref, g_ref, b_ref, of_ref, ob_ref, sorted_ref, sem,
                    *, n_chunks_max):
    blk = pl.program_id(0)
    n = nch_ref[blk]

    def copy(c):
        return _chunk_copy(ys_ref, dst_ref[blk * n_chunks_max + c], sorted_ref, c, sem)

    def start(c, carry):
        copy(c).start()
        return carry

    def clear(c, carry):
        sorted_ref[pl.ds(pl.multiple_of(c * BF16_ROWS, BF16_ROWS), BF16_ROWS), :] = jnp.zeros(
            (BF16_ROWS, sorted_ref.shape[1]), sorted_ref.dtype)
        return carry

    def wait(c, carry):
        copy(c).wait()
        return carry

    lax.fori_loop(0, n, start, 0)
    lax.fori_loop(n, n_chunks_max, clear, 0)
    lax.fori_loop(0, n, wait, 0)
    meta = meta_ref[...]
    col = lax.broadcasted_iota(jnp.int32, (meta.shape[0], sorted_ref.shape[0]), 1).astype(F32)
    ys = sorted_ref[...]
    y = (meta[:, 2:3] * _dot((col == meta[:, 0:1]).astype(BF16), ys)
         + meta[:, 3:4] * _dot((col == meta[:, 1:2]).astype(BF16), ys))
    out = _layer_norm(ALPHA * h_ref[...] + y, g_ref[...], b_ref[...])
    of_ref[...] = out
    ob_ref[...] = out.astype(BF16)


def moe_sorted_res_ln(x, h, meta, cnt, w_gu, w_down, g, b, nf=2, lead=()):
    m, d = x.shape
    n_e, ff, _ = w_down.shape[-3:]
    tf = ff // nf
    nb = m // MOE_BLOCK
    n_chunks_max = 2 * MOE_BLOCK // BF16_ROWS + n_e
    n_sorted = n_chunks_max * BF16_ROWS
    cpt = MOE_BLOCK // BF16_ROWS
    n_tiles = -(-nb * n_chunks_max // cpt) + n_e
    dst, n_chunks, tile_expert, n_used = _moe_plan(cnt[:, 0, :n_e].astype(jnp.int32), n_chunks_max, n_tiles)
    rows_t = jnp.swapaxes(meta[:, :SUBLANES].reshape(nb, MOE_BLOCK, SUBLANES), 1, 2)

    any_space = pl.BlockSpec(memory_space=pl.ANY)
    xs = pl.pallas_call(
        functools.partial(_dispatch_kernel, n_chunks_max=n_chunks_max),
        grid_spec=pltpu.PrefetchScalarGridSpec(
            num_scalar_prefetch=2,
            grid=(nb,),
            in_specs=[pl.BlockSpec((MOE_BLOCK, d), lambda i, *_: (i, 0)),
                      pl.BlockSpec((1, SUBLANES, MOE_BLOCK), lambda i, *_: (i, 0, 0)),
                      any_space],
            out_specs=any_space,
            scratch_shapes=[pltpu.VMEM((n_sorted, d), BF16), pltpu.SemaphoreType.DMA(())],
        ),
        out_shape=jax.ShapeDtypeStruct((n_tiles * MOE_BLOCK, d), BF16),
        input_output_aliases={4: 0},
        compiler_params=_params("arbitrary"),
        name="moe_dispatch",
    )(dst, n_chunks, x, rows_t, jnp.zeros((n_tiles * MOE_BLOCK, d), BF16))

    def tile(i, f, te, nt):
        return jnp.maximum(jnp.minimum(i, nt[0] - 1), 0)

    ys = pl.pallas_call(
        _moe_ffn_kernel,
        grid_spec=pltpu.PrefetchScalarGridSpec(
            num_scalar_prefetch=2,
            grid=(n_tiles, nf),
            in_specs=[pl.BlockSpec((MOE_BLOCK, d), lambda i, f, te, nt: (tile(i, f, te, nt), 0)),
                      _wspec(lead, (None, d, tf), lambda i, f, te, nt: (te[tile(i, f, te, nt)], 0, f)),
                      _wspec(lead, (None, d, tf), lambda i, f, te, nt: (te[tile(i, f, te, nt)], 0, nf + f)),
                      _wspec(lead, (None, tf, d), lambda i, f, te, nt: (te[tile(i, f, te, nt)], f, 0))],
            out_specs=pl.BlockSpec((MOE_BLOCK, d), lambda i, f, te, nt: (i, 0)),
            scratch_shapes=[pltpu.VMEM((MOE_BLOCK, d), F32)],
        ),
        out_shape=jax.ShapeDtypeStruct((n_tiles * MOE_BLOCK, d), BF16),
        compiler_params=_params("arbitrary", "arbitrary"),
        name="moe_expert_ffn",
    )(tile_expert, n_used, xs, w_gu, w_gu, w_down)

    row = lambda i, *_: (i, 0)
    fixed = lambda i, *_: (0, 0)
    return pl.pallas_call(
        functools.partial(_combine_kernel, n_chunks_max=n_chunks_max),
        grid_spec=pltpu.PrefetchScalarGridSpec(
            num_scalar_prefetch=2,
            grid=(nb,),
            in_specs=[any_space, pl.BlockSpec((MOE_BLOCK, LANES), row), pl.BlockSpec((MOE_BLOCK, d), row),
                      pl.BlockSpec((1, d), fixed), pl.BlockSpec((1, d), fixed)],
            out_specs=[pl.BlockSpec((MOE_BLOCK, d), row), pl.BlockSpec((MOE_BLOCK, d), row)],
            scratch_shapes=[pltpu.VMEM((n_sorted, d), BF16), pltpu.SemaphoreType.DMA(())],
        ),
        out_shape=[jax.ShapeDtypeStruct((m, d), F32), jax.ShapeDtypeStruct((m, d), BF16)],
        compiler_params=_params("arbitrary"),
        name="moe_combine_res_ln",
    )(dst, n_chunks, ys, meta, h, g, b)


def _log_sigmoid(x):
    return jnp.minimum(x, 0.0) - jnp.log(1.0 + jnp.exp(-jnp.abs(x)))


def _mlstm_kernel(q_ref, k_ref, v_ref, o_ref, gc_ref, gr_ref, bc_ref, br_ref, gh_ref, c0_ref, n0_ref, m0_ref,
                  hg_ref, c_ref, n_ref, m_ref, *, chunk, t_valid, nh, dh):
    ci = pl.program_id(1)

    @pl.when(ci == 0)
    def _():
        c_ref[...] = c0_ref[...]
        n_ref[...] = n0_ref[...]
        m_ref[...] = m0_ref[...]

    gc = gc_ref[0] + bc_ref[...]
    gr = gr_ref[0] + br_ref[...]
    li_c, lf_c = gc[:, :nh], _log_sigmoid(gc[:, nh:])
    li_r, lf_r = gr[:nh, :], _log_sigmoid(gr[nh:, :])
    if t_valid < chunk:
        tok_c = lax.broadcasted_iota(jnp.int32, li_c.shape, 0) < t_valid
        tok_r = lax.broadcasted_iota(jnp.int32, li_r.shape, 1) < t_valid
        li_c, lf_c = jnp.where(tok_c, li_c, NEG), jnp.where(tok_c, lf_c, 0.0)
        li_r, lf_r = jnp.where(tok_r, li_r, NEG), jnp.where(tok_r, lf_r, 0.0)
    row = lax.broadcasted_iota(jnp.int32, (chunk, chunk), 0)
    col = lax.broadcasted_iota(jnp.int32, (chunk, chunk), 1)
    causal = col <= row
    hi = lax.Precision.HIGHEST
    b_c = jnp.dot(causal.astype(F32), lf_c, preferred_element_type=F32, precision=hi)
    b_r = jnp.dot(lf_r, (row <= col).astype(F32), preferred_element_type=F32, precision=hi)

    mm = q_ref.dtype
    staged = []
    for h in range(nh):
        hs = slice(h * dh, (h + 1) * dh)
        bcol, icol = b_c[:, h:h + 1], li_c[:, h:h + 1]
        brow, irow = b_r[h:h + 1, :], li_r[h:h + 1, :]
        m_prev = m_ref[0, h, :, 0:1]
        c_prev = c_ref[0, h]
        q = q_ref[:, hs]
        kf = k_ref[:, hs].astype(F32) * (dh ** -0.5)
        dmat = jnp.where(causal, bcol - brow + irow, NEG)
        inter = bcol + m_prev
        mt = jnp.maximum(inter, jnp.max(dmat, axis=1, keepdims=True))
        w_inter = jnp.exp(inter - mt)
        qk = _dot_nt(q, kf.astype(mm))
        qc = _dot(q, c_prev.astype(mm))
        staged.append((hs, bcol, icol, m_prev, c_prev, q, kf, dmat, mt, w_inter, qk, qc))

    readout = []
    for h, (hs, bcol, icol, m_prev, c_prev, q, kf, dmat, mt, w_inter, qk, qc) in enumerate(staged):
        s = qk * jnp.exp(dmat - mt)
        num = w_inter * qc + _dot(s.astype(mm), v_ref[:, hs])
        den = (w_inter * jnp.sum(q.astype(F32) * n_ref[0, h], axis=1, keepdims=True)
               + jnp.sum(s, axis=1, keepdims=True))
        readout.append((num, den))

    for h, (num, den) in enumerate(readout):
        hs, mt = staged[h][0], staged[h][8]
        hh = num * (1.0 / jnp.maximum(jnp.abs(den), jnp.exp(-mt)))
        mu = jnp.mean(hh, axis=1, keepdims=True)
        hc = hh - mu
        var = jnp.mean(hc * hc, axis=1, keepdims=True)
        hn = hc * lax.rsqrt(var + LN_EPS) * gh_ref[h:h + 1, :]
        hg_ref[:, hs] = (hn * _sigmoid(o_ref[:, hs])).astype(hg_ref.dtype)

    for h, (hs, bcol, icol, m_prev, c_prev, q, kf, dmat, mt, w_inter, qk, qc) in enumerate(staged):
        n_prev = n_ref[0, h]
        v = v_ref[:, hs]
        b_last = bcol[chunk - 1:chunk, :]
        gdec = b_last - bcol + icol
        m_new = jnp.maximum(b_last + m_prev, jnp.max(gdec, axis=0, keepdims=True))
        decay = jnp.exp(b_last + m_prev - m_new)
        kw = kf * jnp.exp(gdec - m_new)
        c_ref[0, h] = decay * c_prev + _dot_tn(kw.astype(mm), v)
        n_ref[0, h] = decay * n_prev + jnp.sum(kw, axis=0, keepdims=True)
        m_ref[0, h] = jnp.broadcast_to(m_new, (1, LANES))


def mlstm(qkv, o, gates, b_gate, g_head, c0, n0, m0, t_valid, chunk):
    bsz, t_len, d3 = qkv.shape
    d = d3 // 3
    nh = g_head.shape[0]
    dh = d // nh
    nc = t_len // chunk
    gates_t = jnp.swapaxes(gates, 1, 2)
    seq = lambda b, c: (b, 0, 0, 0)
    fixed = lambda b, c: (0, 0)
    kern = functools.partial(_mlstm_kernel, chunk=chunk, t_valid=t_valid, nh=nh, dh=dh)
    hg, c, n, m = pl.pallas_call(
        kern,
        grid=(bsz, nc),
        in_specs=[pl.BlockSpec((None, chunk, d), lambda b, c: (b, c, 0)),
                  pl.BlockSpec((None, chunk, d), lambda b, c: (b, c, 1)),
                  pl.BlockSpec((None, chunk, d), lambda b, c: (b, c, 2)),
                  pl.BlockSpec((None, chunk, d), lambda b, c: (b, c, 0)),
                  pl.BlockSpec((1, chunk, 2 * nh), lambda b, c: (b, c, 0)),
                  pl.BlockSpec((1, 2 * nh, chunk), lambda b, c: (b, 0, c)),
                  pl.BlockSpec((1, 2 * nh), fixed), pl.BlockSpec((2 * nh, 1), fixed),
                  pl.BlockSpec((nh, dh), fixed),
                  pl.BlockSpec((1, nh, dh, dh), seq), pl.BlockSpec((1, nh, 1, dh), seq),
                  pl.BlockSpec((1, nh, 1, LANES), seq)],
        out_specs=[pl.BlockSpec((None, chunk, d), lambda b, c: (b, c, 0)),
                   pl.BlockSpec((1, nh, dh, dh), seq), pl.BlockSpec((1, nh, 1, dh), seq),
                   pl.BlockSpec((1, nh, 1, LANES), seq)],
        out_shape=[jax.ShapeDtypeStruct((bsz, t_len, d), qkv.dtype),
                   jax.ShapeDtypeStruct((bsz, nh, dh, dh), F32),
                   jax.ShapeDtypeStruct((bsz, nh, 1, dh), F32),
                   jax.ShapeDtypeStruct((bsz, nh, 1, LANES), F32)],
        compiler_params=_params("parallel", "arbitrary"),
        name="mlstm",
    )(qkv, qkv, qkv, o, gates, gates_t, b_gate.reshape(1, 2 * nh), b_gate.reshape(2 * nh, 1), g_head,
      c0, n0.reshape(bsz, nh, 1, dh), jnp.broadcast_to(m0[:, :, None, None], (bsz, nh, 1, LANES)))
    return hg, c, n.reshape(bsz, nh, dh), m[:, :, 0, 0]


def _lambda_value(lam_ref, lam_init):
    lp = lam_ref[...]
    a = jnp.sum(lp[0:1, :] * lp[1:2, :], axis=1, keepdims=True)
    b = jnp.sum(lp[2:3, :] * lp[3:4, :], axis=1, keepdims=True)
    return jnp.exp(a) - jnp.exp(b) + lam_init


def _split_queries(q, dh):
    lane = lax.broadcasted_iota(jnp.int32, q.shape, 1)
    zero = jnp.zeros_like(q)
    return jnp.concatenate([jnp.where(lane < dh, q, zero), jnp.where(lane >= dh, q, zero)], axis=0)


def _flash_step(qi, q_ref, k_ref, vt_ref, lam_ref, g_ref, o_ref, qs_ref, s_ref, m_ref, l_ref, acc_ref,
                *, blk, dh, lam_init, interleave=()):
    qs_ref[...] = _split_queries(q_ref[...], dh)
    m_ref[...] = jnp.full(m_ref.shape, NEG, F32)
    l_ref[...] = jnp.zeros(l_ref.shape, F32)
    acc_ref[...] = jnp.zeros(acc_ref.shape, F32)
    sub = 2 * blk // ATTN_QSPLIT

    def scores(ki, c):
        k = k_ref[pl.ds(pl.multiple_of(ki * blk, blk), blk), :]
        return _dot_nt(k, qs_ref[c * sub:(c + 1) * sub, :])

    def update(ki, c, s, diagonal):
        cs = slice(c * sub, (c + 1) * sub)
        vt = vt_ref[:, pl.ds(pl.multiple_of(ki * blk, blk), blk)]
        if diagonal:
            key = lax.broadcasted_iota(jnp.int32, s.shape, 0)
            qry = (lax.broadcasted_iota(jnp.int32, s.shape, 1) + c * sub) % blk
            s = jnp.where(key <= qry, s, NEG)
        m_prev = m_ref[:, cs]
        m_new = jnp.maximum(m_prev, jnp.max(s, axis=0, keepdims=True))
        alpha = jnp.exp2(m_prev - m_new)
        p = jnp.exp2(s - m_new)
        l_ref[:, cs] = alpha * l_ref[:, cs] + jnp.sum(p, axis=0, keepdims=True)
        acc_ref[:, cs] = alpha * acc_ref[:, cs] + _dot(vt, p.astype(BF16))
        m_ref[:, cs] = m_new

    def block(ki, diagonal):
        pending = [s_ref[i] for i in range(ATTN_LOOKAHEAD)]
        for c in range(ATTN_QSPLIT):
            nxt = c + ATTN_LOOKAHEAD
            if nxt < ATTN_QSPLIT:
                pending.append(scores(ki, nxt))
            elif not diagonal:
                pending.append(scores(ki + 1, nxt - ATTN_QSPLIT))
            if diagonal and c < len(interleave):
                interleave[c]()
            update(ki, c, pending.pop(0), diagonal)
        if diagonal:
            for extra in interleave[ATTN_QSPLIT:]:
                extra()
        for i, s in enumerate(pending):
            s_ref[i] = s

    def body(ki, carry):
        block(ki, False)
        return carry

    for i in range(ATTN_LOOKAHEAD):
        s_ref[i] = scores(0, i)
    lax.fori_loop(0, qi, body, 0)
    block(qi, True)
    lam = _lambda_value(lam_ref, lam_init)
    acc = acc_ref[...]
    r = 1.0 / l_ref[...]
    o = acc[:, :blk] * r[:, :blk] - lam * (acc[:, blk:] * r[:, blk:])
    ms = jnp.mean(o * o, axis=0, keepdims=True)
    o = o * lax.rsqrt(ms + LN_EPS) * (g_ref[...] * (1.0 - lam_init))
    o_ref[...] = o.T.astype(o_ref.dtype)


def _decode_stages(j, n_steps, q_ref, k_refs, v_refs, kn_ref, vn_ref, lam_ref, g_ref, o_ref, qs_ref, m_ref, l_ref,
                   acc_ref, s_ref, p_ref, alpha_ref, *, nh, dh, lam_init):
    page = k_refs[0].shape[1]
    rows = page * nh

    def begin():
        @pl.when(j == 0)
        def _():
            qs = _split_queries(q_ref[0], dh) * (dh ** -0.5)
            qs_ref[...] = qs.astype(BF16)
            kn = jnp.concatenate([kn_ref[0, 0], kn_ref[0, 0]], axis=0)
            m_ref[...] = jnp.sum(qs * kn, axis=1, keepdims=True)
            l_ref[...] = jnp.ones(l_ref.shape, F32)
            acc_ref[...] = jnp.concatenate([vn_ref[0, 0], vn_ref[0, 0]], axis=0)

    def scores():
        qs = qs_ref[...]
        s = jnp.concatenate([_dot_nt(qs, k_ref[0].reshape(rows, 2 * dh).astype(BF16)) for k_ref in k_refs], axis=1)
        row = lax.broadcasted_iota(jnp.int32, s.shape, 0) % nh
        col = lax.broadcasted_iota(jnp.int32, s.shape, 1) % nh
        s_ref[...] = jnp.where(row == col, s, NEG)

    def softmax():
        s = s_ref[...]
        m_prev = m_ref[...]
        m_new = jnp.maximum(m_prev, jnp.max(s, axis=1, keepdims=True))
        alpha = jnp.exp(m_prev - m_new)
        p = jnp.exp(s - m_new).astype(BF16)
        l_ref[...] = alpha * l_ref[...] + jnp.sum(p.astype(F32), axis=1, keepdims=True)
        m_ref[...] = m_new
        alpha_ref[...] = alpha
        p_ref[...] = p

    def accumulate():
        pv = _dot(p_ref[:, :rows], v_refs[0][0].reshape(rows, 2 * dh).astype(BF16))
        for i in range(1, len(v_refs)):
            pv += _dot(p_ref[:, i * rows:(i + 1) * rows], v_refs[i][0].reshape(rows, 2 * dh).astype(BF16))
        acc_ref[...] = alpha_ref[...] * acc_ref[...] + pv

    def finish():
        @pl.when(j == n_steps - 1)
        def _():
            lam = _lambda_value(lam_ref, lam_init)
            acc = acc_ref[...]
            r = 1.0 / l_ref[...]
            o = acc[:nh] * r[:nh] - lam * (acc[nh:] * r[nh:])
            ms = jnp.mean(o * o, axis=1, keepdims=True)
            o_ref[0] = o * lax.rsqrt(ms + LN_EPS) * g_ref[...] * (1.0 - lam_init)

    return begin, scores, softmax, accumulate, finish


def _attention_kernel(pt_ref, q_ref, k_ref, vt_ref, lam_ref, gcol_ref, qd_ref, *refs,
                      blk, dh, nh, lam_init, pps, steps_per_seq, n_decode_steps, n_grid_steps):
    del pt_ref
    kd_refs, vd_refs = refs[:pps], refs[pps:2 * pps]
    (kn_ref, vn_ref, grow_ref, o_ref, od_ref,
     qs_ref, s_ref, m_ref, l_ref, acc_ref, dqs_ref, dm_ref, dl_ref, dacc_ref, ds_ref, dp_ref, dalpha_ref) = refs[2 * pps:]
    step = (pl.program_id(0) * pl.num_programs(1) + pl.program_id(1)) * pl.num_programs(2) + pl.program_id(2)
    stages = _decode_stages(step % steps_per_seq, steps_per_seq, qd_ref, kd_refs, vd_refs, kn_ref, vn_ref, lam_ref,
                            grow_ref, od_ref, dqs_ref, dm_ref, dl_ref, dacc_ref, ds_ref, dp_ref, dalpha_ref,
                            nh=nh, dh=dh, lam_init=lam_init)
    if n_decode_steps != n_grid_steps:
        stages = [functools.partial(pl.when(step < n_decode_steps), f) for f in stages]
    begin, scores, softmax, accumulate, finish = stages
    begin()
    _flash_step(pl.program_id(2), q_ref, k_ref, vt_ref, lam_ref, gcol_ref, o_ref, qs_ref, s_ref, m_ref, l_ref,
                acc_ref, blk=blk, dh=dh, lam_init=lam_init, interleave=(scores, softmax, accumulate))
    finish()


def diff_attention(q, k, vt, q_dec, k_new, v_new, cache_k, cache_v, page_table, lam_p, g_subln, lam_init,
                   blk=ATTN_BLOCK):
    bsz, t_len, d = q.shape
    n_seq, nh, hw = q_dec.shape
    dh = hw // 2
    blk = min(blk, t_len)
    nq = t_len // blk
    n_steps = bsz * nh * nq
    n_pages = page_table.shape[1]
    page = cache_k.shape[1]
    pps = min(p for p in range(1, n_pages + 1) if n_pages % p == 0 and n_seq * (n_pages // p) <= n_steps)
    steps_per_seq = n_pages // pps
    n_decode_steps = n_seq * steps_per_seq

    def decode_pos(b, h, i):
        step = jnp.minimum((b * nh + h) * nq + i, n_decode_steps - 1)
        return step // steps_per_seq, step % steps_per_seq

    def seq3(b, h, i, pt):
        return (decode_pos(b, h, i)[0], 0, 0)

    def seq4(b, h, i, pt):
        return (decode_pos(b, h, i)[0], 0, 0, 0)

    def paged(p):
        def index(b, h, i, pt):
            s, j = decode_pos(b, h, i)
            return (pt[s, j * pps + p], 0, 0, 0)
        return pl.BlockSpec((1, page, nh, hw), index)

    fixed = lambda b, h, i, pt: (0, 0)
    grid_spec = pltpu.PrefetchScalarGridSpec(
        num_scalar_prefetch=1,
        grid=(bsz, nh, nq),
        in_specs=([pl.BlockSpec((None, blk, hw), lambda b, h, i, pt: (b, i, h)),
                   pl.BlockSpec((None, t_len, hw), lambda b, h, i, pt: (b, 0, h)),
                   pl.BlockSpec((None, hw, t_len), lambda b, h, i, pt: (b, h, 0)),
                   pl.BlockSpec(lam_p.shape, fixed), pl.BlockSpec((hw, 1), fixed),
                   pl.BlockSpec((1, nh, hw), seq3)]
                  + [paged(p) for p in range(pps)] * 2
                  + [pl.BlockSpec((1, 1, nh, hw), seq4), pl.BlockSpec((1, 1, nh, hw), seq4),
                     pl.BlockSpec((1, hw), fixed)]),
        out_specs=[pl.BlockSpec((None, blk, hw), lambda b, h, i, pt: (b, i, h)),
                   pl.BlockSpec((1, nh, hw), seq3)],
        scratch_shapes=[pltpu.VMEM((2 * blk, hw), BF16),
                        pltpu.VMEM((ATTN_LOOKAHEAD, blk, 2 * blk // ATTN_QSPLIT), F32),
                        pltpu.VMEM((1, 2 * blk), F32), pltpu.VMEM((1, 2 * blk), F32), pltpu.VMEM((hw, 2 * blk), F32),
                        pltpu.VMEM((2 * nh, hw), BF16), pltpu.VMEM((2 * nh, 1), F32),
                        pltpu.VMEM((2 * nh, 1), F32), pltpu.VMEM((2 * nh, hw), F32),
                        pltpu.VMEM((2 * nh, pps * page * nh), F32), pltpu.VMEM((2 * nh, pps * page * nh), BF16),
                        pltpu.VMEM((2 * nh, 1), F32)],
    )
    return pl.pallas_call(
        functools.partial(_attention_kernel, blk=blk, dh=dh, nh=nh, lam_init=lam_init, pps=pps,
                          steps_per_seq=steps_per_seq, n_decode_steps=n_decode_steps, n_grid_steps=n_steps),
        grid_spec=grid_spec,
        out_shape=[jax.ShapeDtypeStruct((bsz, t_len, d), BF16), jax.ShapeDtypeStruct((n_seq, nh, hw), F32)],
        compiler_params=_params("arbitrary", "arbitrary", "arbitrary"),
        name="diff_attention",
    )(page_table, q, k, vt, lam_p, g_subln.reshape(hw, 1), q_dec, *([cache_k] * pps), *([cache_v] * pps),
      k_new, v_new, g_subln.reshape(1, hw))


def _trunk(x, state, decode, wts):
    bsz, t_len, d = x.shape
    m_rows = bsz * t_len
    n_a = wts["w_in"].shape[0]
    hw = wts["g_subln"].shape[-1]
    n_heads = d // hw
    act = wts["w_in"].dtype
    pick = (lambda h32, hbf: hbf) if act == BF16 else (lambda h32, hbf: h32)
    h32 = x.reshape(m_rows, d)
    ha = h32.astype(act)
    cs, ns, ms = [], [], []
    k32 = v32 = kbf = vt = None
    for i in range(DEPTH):
        ln_g, ln_b = wts["ln_g"][i], wts["ln_b"][i]
        if i < n_a:
            (qkv,) = matmul(ha, wts["w_in"], [act], lead=(i,), n=3 * d)
            (og,) = matmul(ha, wts["w_in"], [F32], lead=(i,), col0=3, n=d)
            (gates,) = matmul(ha, wts["w_gates"], [F32], lead=(i,))
            chunk = MLSTM_CHUNK if t_len % MLSTM_CHUNK == 0 else BF16_ROWS
            t_pad = -(-t_len // chunk) * chunk
            gates = gates[:, :2 * A_HEADS].reshape(bsz, t_len, 2 * A_HEADS)
            og = og.reshape(bsz, t_len, d)
            qkv = qkv.reshape(bsz, t_len, 3 * d)
            if t_pad != t_len:
                pad = ((0, 0), (0, t_pad - t_len), (0, 0))
                qkv, og, gates = jnp.pad(qkv, pad), jnp.pad(og, pad), jnp.pad(gates, pad)
            if state is None:
                dh = d // A_HEADS
                c0 = jnp.zeros((bsz, A_HEADS, dh, dh), F32)
                n0 = jnp.zeros((bsz, A_HEADS, dh), F32)
                m0 = jnp.zeros((bsz, A_HEADS), F32)
            else:
                c0, n0, m0 = state[0][i], state[1][i], state[2][i]
            hg, c, n, m = mlstm(qkv, og, gates, wts["b_gate"][i], wts["g_head"][i], c0, n0, m0, t_len, chunk)
            cs.append(c)
            ns.append(n)
            ms.append(m)
            mix_in = hg[:, :t_len].reshape(m_rows, d)
            w_mix, mix_lead = wts["w_out"], (i,)
        else:
            j = i - n_a
            if j == 0:
                if not decode:
                    k32, kbf = matmul(ha, wts["w_kv"], [F32, BF16], n=d)
                    (v32,) = matmul(ha, wts["w_kv"], [F32], col0=1, n=d)
                    vt = matmul_t(ha.reshape(bsz, t_len, d), wts["w_v_t"])
                else:
                    (k32,) = matmul(ha, wts["w_kv"], [F32], n=d)
                    (v32,) = matmul(ha, wts["w_kv"], [F32], col0=1, n=d)
            if not decode:
                (q,) = matmul(ha, wts["w_q"], [BF16], lead=(j,), scale=math.log2(math.e) * (hw // 2) ** -0.5)
                o = yield {"q": q.reshape(bsz, t_len, d), "k": kbf.reshape(bsz, t_len, d), "vt": vt}
                mix_in = o.reshape(m_rows, d)
            else:
                (q,) = matmul(ha, wts["w_q"], [F32], lead=(j,))
                o = yield {"q": q.reshape(bsz, n_heads, hw), "k_new": k32.reshape(bsz, 1, n_heads, hw),
                           "v_new": v32.reshape(bsz, 1, n_heads, hw)}
                mix_in = o.reshape(m_rows, d).astype(act)
            w_mix, mix_lead = wts["w_o"], (j,)
        h32, hbf = matmul_res_ln(mix_in, w_mix, h32, ln_g[0:1], ln_b[0:1], lead=mix_lead)
        ha = pick(h32, hbf)
        if i % 2 == 0:
            h32, hbf = ffn_res_ln(ha, h32, wts["w_ffn_gu"], wts["w_ffn_down"], ln_g[1:2], ln_b[1:2], lead=(i // 2,))
        else:
            dg, meta, cnt = router(h32, wts["w_router"][i // 2])
            w_gu, w_down, lead = wts["w_moe_gu"], wts["w_moe_down"], (i // 2,)
            if m_rows >= MOE_MIN_ROWS and m_rows % MOE_BLOCK == 0:
                h32, hbf = moe_sorted_res_ln(hbf, h32, meta, cnt, w_gu, w_down, ln_g[1:2], ln_b[1:2], lead=lead)
            else:
                h32, hbf = moe_dense_res_ln(hbf, h32, dg, w_gu, w_down, ln_g[1:2], ln_b[1:2], lead=lead)
        ha = pick(h32, hbf)
    y = h32.reshape(bsz, t_len, d)
    k_rows = k32.reshape(bsz, t_len, n_heads, hw)
    v_rows = v32.reshape(bsz, t_len, n_heads, hw)
    return y, jnp.stack(cs), jnp.stack(ns), jnp.stack(ms), k_rows, v_rows


def _resume(gen, value):
    try:
        return gen.send(value), None
    except StopIteration as done:
        return None, done.value


def kernel(x_prompt, x_sample, state_c, state_n, state_m, cache_k, cache_v, page_table, ln_g, ln_b, w_in_a, b_gate_a, g_head_a, w_out_a, w_kv, w_q_b, lam_b, g_subln_b, w_o_b, w_ffn_gu, w_ffn_down, w_router, w_moe_gu, w_moe_down):
    d = x_prompt.shape[-1]
    n_gate = w_in_a.shape[-1] - 4 * d
    w_gates = jnp.pad(w_in_a[:, :, 4 * d:], ((0, 0), (0, 0), (0, LANES - n_gate)))

    moe_bf16 = {"w_moe_gu": w_moe_gu.astype(BF16), "w_moe_down": w_moe_down.astype(BF16)}

    def weights(cast):
        return {
            "ln_g": ln_g, "ln_b": ln_b, "b_gate": b_gate_a, "g_head": g_head_a, "w_router": w_router,
            "g_subln": g_subln_b,
            "w_in": cast(w_in_a), "w_gates": cast(w_gates), "w_out": cast(w_out_a), "w_kv": cast(w_kv),
            "w_v_t": cast(w_kv[:, d:].T), "w_q": cast(w_q_b), "w_o": cast(w_o_b), "w_ffn_gu": cast(w_ffn_gu),
            "w_ffn_down": cast(w_ffn_down), **moe_bf16,
        }

    prompt = _trunk(x_prompt, None, False, weights(lambda a: a.astype(BF16)))
    sample = _trunk(x_sample, (state_c, state_n, state_m), True, weights(lambda a: a))
    n_a = w_in_a.shape[0]
    req_p, req_s = next(prompt), next(sample)
    out_p = out_s = None
    for j in range(w_q_b.shape[0]):
        o_p, o_s = diff_attention(req_p["q"], req_p["k"], req_p["vt"], req_s["q"], req_s["k_new"], req_s["v_new"],
                                  cache_k, cache_v, page_table, lam_b[j], g_subln_b[j], _lambda_init(n_a + j))
        req_p, out_p = _resume(prompt, o_p)
        req_s, out_s = _resume(sample, o_s)
    y_p, c_p, n_p, m_p, k_p, v_p = out_p
    y_s, c_s, n_s, m_s, k_s, v_s = out_s
    return (y_p, y_s, c_p, n_p, m_p, k_p, v_p, c_s, n_s, m_s, k_s, v_s)
```

```python
import functools
import math

import jax
import jax.numpy as jnp
from jax import lax
from jax.experimental import pallas as pl
from jax.experimental.pallas import tpu as pltpu

F32 = jnp.float32
BF16 = jnp.bfloat16

DEPTH = 4
ALPHA = (2 * DEPTH) ** 0.25
LN_EPS = 1e-5
NEG = -1e30
LANES = 128
SUBLANES = 8
BF16_ROWS = 16
VMEM_LIMIT = 56 * 1024 * 1024

A_HEADS = 4
MLSTM_CHUNK = 256
ATTN_BLOCK = 512
ATTN_QSPLIT = 2
ATTN_LOOKAHEAD = 2
MOE_BLOCK = 512
MOE_MIN_ROWS = 1024


def _lambda_init(layer_idx):
    return 0.8 - 0.6 * math.exp(-0.3 * layer_idx)


def _params(*sem):
    return pltpu.CompilerParams(dimension_semantics=sem, vmem_limit_bytes=VMEM_LIMIT)


def _contract(a, b, dims):
    precision = lax.Precision.HIGHEST if a.dtype == F32 and b.dtype == F32 else None
    return lax.dot_general(a, b, (dims, ((), ())), preferred_element_type=F32, precision=precision)


def _dot(a, b):
    return _contract(a, b, ((1,), (0,)))


def _dot_nt(a, b):
    return _contract(a, b, ((1,), (1,)))


def _dot_tn(a, b):
    return _contract(a, b, ((0,), (0,)))


def _layer_norm(y, g, b):
    mu = jnp.mean(y, axis=-1, keepdims=True)
    yc = y - mu
    var = jnp.mean(yc * yc, axis=-1, keepdims=True)
    return yc * lax.rsqrt(var + LN_EPS) * g + b


def _sigmoid(x):
    return 0.5 * jnp.tanh(0.5 * x) + 0.5


def _row_tile(m, want):
    return want if m % want == 0 else m


def _wspec(lead, block, index):
    return pl.BlockSpec((None,) * len(lead) + block, lambda *a: tuple(lead) + index(*a))


def _mm_kernel(x_ref, w_ref, *o_refs, scale):
    acc = _dot(x_ref[...], w_ref[...])
    if scale != 1.0:
        acc = acc * scale
    for o_ref in o_refs:
        o_ref[...] = acc.astype(o_ref.dtype)


def matmul(x, w, out_dtypes, tm=1024, tn=1024, scale=1.0, lead=(), col0=0, n=None):
    m, k = x.shape
    n = w.shape[-1] if n is None else n
    tm = _row_tile(m, tm)
    tn = _row_tile(n, tn)
    return pl.pallas_call(
        functools.partial(_mm_kernel, scale=scale),
        grid=(m // tm, n // tn),
        in_specs=[pl.BlockSpec((tm, k), lambda i, j: (i, 0)),
                  _wspec(lead, (k, tn), lambda i, j: (0, col0 + j))],
        out_specs=[pl.BlockSpec((tm, tn), lambda i, j: (i, j)) for _ in out_dtypes],
        out_shape=[jax.ShapeDtypeStruct((m, n), d) for d in out_dtypes],
        compiler_params=_params("parallel", "parallel"),
        name="matmul",
    )(x, w)


def _mm_t_kernel(x_ref, wt_ref, o_ref):
    o_ref[...] = _dot_nt(wt_ref[...], x_ref[...]).astype(o_ref.dtype)


def matmul_t(x, wt, tm=1024, tn=1024):
    bsz, t_len, k = x.shape
    n = wt.shape[0]
    tm = _row_tile(t_len, tm)
    tn = _row_tile(n, tn)
    return pl.pallas_call(
        _mm_t_kernel,
        grid=(bsz, t_len // tm, n // tn),
        in_specs=[pl.BlockSpec((None, tm, k), lambda b, i, j: (b, i, 0)),
                  pl.BlockSpec((tn, k), lambda b, i, j: (j, 0))],
        out_specs=pl.BlockSpec((None, tn, tm), lambda b, i, j: (b, j, i)),
        out_shape=jax.ShapeDtypeStruct((bsz, n, t_len), BF16),
        compiler_params=_params("parallel", "parallel", "parallel"),
        name="matmul_t",
    )(x, wt)


def _mm_res_ln_kernel(x_ref, w_ref, h_ref, g_ref, b_ref, of_ref, ob_ref):
    y = ALPHA * h_ref[...] + _dot(x_ref[...], w_ref[...])
    out = _layer_norm(y, g_ref[...], b_ref[...])
    of_ref[...] = out
    ob_ref[...] = out.astype(BF16)


def matmul_res_ln(x, w, h, g, b, tm=1024, lead=()):
    m, k = x.shape
    d = w.shape[-1]
    tm = _row_tile(m, tm)
    row = lambda i: (i, 0)
    fixed = lambda i: (0, 0)
    return pl.pallas_call(
        _mm_res_ln_kernel,
        grid=(m // tm,),
        in_specs=[pl.BlockSpec((tm, k), row), _wspec(lead, (k, d), fixed), pl.BlockSpec((tm, d), row),
                  pl.BlockSpec((1, d), fixed), pl.BlockSpec((1, d), fixed)],
        out_specs=[pl.BlockSpec((tm, d), row), pl.BlockSpec((tm, d), row)],
        out_shape=[jax.ShapeDtypeStruct((m, d), F32), jax.ShapeDtypeStruct((m, d), BF16)],
        compiler_params=_params("parallel"),
        name="matmul_res_ln",
    )(x, w, h, g, b)


def _swiglu_part(x, wg_ref, wu_ref, wd_ref):
    gate = _dot(x, wg_ref[...])
    up = _dot(x, wu_ref[...])
    act = (gate * _sigmoid(gate) * up).astype(wd_ref.dtype)
    return _dot(act, wd_ref[...])


def _ffn_kernel(x_ref, h_ref, wg_ref, wu_ref, wd_ref, g_ref, b_ref, of_ref, ob_ref, acc_ref):
    f = pl.program_id(1)
    part = _swiglu_part(x_ref[...], wg_ref, wu_ref, wd_ref)

    @pl.when(f == 0)
    def _():
        acc_ref[...] = part

    @pl.when(f > 0)
    def _():
        acc_ref[...] += part

    @pl.when(f == pl.num_programs(1) - 1)
    def _():
        out = _layer_norm(ALPHA * h_ref[...] + acc_ref[...], g_ref[...], b_ref[...])
        of_ref[...] = out
        ob_ref[...] = out.astype(BF16)


def ffn_res_ln(x, h, w_gu, w_down, g, b, tm=512, nf=2, lead=()):
    m, d = x.shape
    ff = w_down.shape[-2]
    tf = ff // nf
    tm = _row_tile(m, tm)
    row = lambda i, f: (i, 0)
    fixed = lambda i, f: (0, 0)
    return pl.pallas_call(
        _ffn_kernel,
        grid=(m // tm, nf),
        in_specs=[pl.BlockSpec((tm, d), row), pl.BlockSpec((tm, d), row),
                  _wspec(lead, (d, tf), lambda i, f: (0, f)),
                  _wspec(lead, (d, tf), lambda i, f: (0, nf + f)),
                  _wspec(lead, (tf, d), lambda i, f: (f, 0)),
                  pl.BlockSpec((1, d), fixed), pl.BlockSpec((1, d), fixed)],
        out_specs=[pl.BlockSpec((tm, d), row), pl.BlockSpec((tm, d), row)],
        out_shape=[jax.ShapeDtypeStruct((m, d), F32), jax.ShapeDtypeStruct((m, d), BF16)],
        scratch_shapes=[pltpu.VMEM((tm, d), F32)],
        compiler_params=_params("parallel", "arbitrary"),
        name="ffn_res_ln",
    )(x, h, w_gu, w_gu, w_down, g, b)


def _router_kernel(x_ref, w_ref, dg_ref, meta_ref, cnt_ref, *, n_experts):
    logits = jnp.dot(x_ref[...], w_ref[...], preferred_element_type=F32, precision=lax.Precision.HIGHEST)
    lane = lax.broadcasted_iota(jnp.int32, logits.shape, 1)
    lg = jnp.where(lane < n_experts, logits, -jnp.inf)
    v1 = jnp.max(lg, axis=1, keepdims=True)
    i1 = jnp.min(jnp.where(lg == v1, lane, LANES), axis=1, keepdims=True)
    lg2 = jnp.where(lane == i1, -jnp.inf, lg)
    v2 = jnp.max(lg2, axis=1, keepdims=True)
    i2 = jnp.min(jnp.where(lg2 == v2, lane, LANES), axis=1, keepdims=True)
    e = jnp.exp(v2 - v1)
    g1 = 1.0 / (1.0 + e)
    g2 = e / (1.0 + e)
    pick1, pick2 = lane == i1, lane == i2
    dg_ref[...] = jnp.where(pick1, g1, 0.0) + jnp.where(pick2, g2, 0.0)

    rows = logits.shape[0]
    onehot = jnp.logical_or(pick1, pick2).astype(BF16)
    r = lax.broadcasted_iota(jnp.int32, (rows, rows), 0)
    c = lax.broadcasted_iota(jnp.int32, (rows, rows), 1)
    rank = _dot((c < r).astype(BF16), onehot)
    cnt = jnp.sum(onehot.astype(F32), axis=0, keepdims=True)
    padded = jnp.floor((cnt + (BF16_ROWS - 1)) * (1.0 / BF16_ROWS)) * BF16_ROWS
    er = lax.broadcasted_iota(jnp.int32, (LANES, LANES), 0)
    ec = lax.broadcasted_iota(jnp.int32, (LANES, LANES), 1)
    padded_rows = jnp.broadcast_to(padded, (BF16_ROWS, LANES)).astype(BF16)
    start = _dot(padded_rows, (er < ec).astype(BF16))[0:1, :]
    sorted_row = start + rank
    row1 = jnp.sum(jnp.where(pick1, sorted_row, 0.0), axis=1, keepdims=True)
    row2 = jnp.sum(jnp.where(pick2, sorted_row, 0.0), axis=1, keepdims=True)
    meta_ref[...] = (jnp.where(lane == 0, row1, 0.0) + jnp.where(lane == 1, row2, 0.0)
                     + jnp.where(lane == 2, g1, 0.0) + jnp.where(lane == 3, g2, 0.0))
    cnt_ref[0] = jnp.broadcast_to(cnt, (SUBLANES, LANES))


def router(h, w_router, tm=MOE_BLOCK):
    m, d = h.shape
    e = w_router.shape[1]
    w = jnp.pad(w_router, ((0, 0), (0, LANES - e)))
    tm = _row_tile(m, tm)
    row = lambda i: (i, 0)
    return pl.pallas_call(
        functools.partial(_router_kernel, n_experts=e),
        grid=(m // tm,),
        in_specs=[pl.BlockSpec((tm, d), row), pl.BlockSpec((d, LANES), lambda i: (0, 0))],
        out_specs=[pl.BlockSpec((tm, LANES), row), pl.BlockSpec((tm, LANES), row),
                   pl.BlockSpec((1, SUBLANES, LANES), lambda i: (i, 0, 0))],
        out_shape=[jax.ShapeDtypeStruct((m, LANES), F32), jax.ShapeDtypeStruct((m, LANES), F32),
                   jax.ShapeDtypeStruct((m // tm, SUBLANES, LANES), F32)],
        compiler_params=_params("parallel"),
        name="router",
    )(h, w)


def _moe_dense_kernel(x_ref, h_ref, dg_ref, wg_ref, wu_ref, wd_ref, g_ref, b_ref, of_ref, ob_ref, acc_ref):
    e = pl.program_id(1)
    f = pl.program_id(2)
    dg = dg_ref[...]
    lane = lax.broadcasted_iota(jnp.int32, dg.shape, 1)
    w_e = jnp.sum(jnp.where(lane == e, dg, 0.0), axis=1, keepdims=True)
    part = w_e * _swiglu_part(x_ref[...], wg_ref, wu_ref, wd_ref)
    first = jnp.logical_and(e == 0, f == 0)

    @pl.when(first)
    def _():
        acc_ref[...] = part

    @pl.when(jnp.logical_not(first))
    def _():
        acc_ref[...] += part

    @pl.when(jnp.logical_and(e == pl.num_programs(1) - 1, f == pl.num_programs(2) - 1))
    def _():
        out = _layer_norm(ALPHA * h_ref[...] + acc_ref[...], g_ref[...], b_ref[...])
        of_ref[...] = out
        ob_ref[...] = out.astype(BF16)


def moe_dense_res_ln(x, h, dg, w_gu, w_down, g, b, tm=512, nf=2, lead=()):
    m, d = x.shape
    n_e, ff, _ = w_down.shape[-3:]
    tf = ff // nf
    tm = _row_tile(m, tm)
    row = lambda i, e, f: (i, 0)
    fixed = lambda i, e, f: (0, 0)
    return pl.pallas_call(
        _moe_dense_kernel,
        grid=(m // tm, n_e, nf),
        in_specs=[pl.BlockSpec((tm, d), row), pl.BlockSpec((tm, d), row), pl.BlockSpec((tm, LANES), row),
                  _wspec(lead, (None, d, tf), lambda i, e, f: (e, 0, f)),
                  _wspec(lead, (None, d, tf), lambda i, e, f: (e, 0, nf + f)),
                  _wspec(lead, (None, tf, d), lambda i, e, f: (e, f, 0)),
                  pl.BlockSpec((1, d), fixed), pl.BlockSpec((1, d), fixed)],
        out_specs=[pl.BlockSpec((tm, d), row), pl.BlockSpec((tm, d), row)],
        out_shape=[jax.ShapeDtypeStruct((m, d), F32), jax.ShapeDtypeStruct((m, d), BF16)],
        scratch_shapes=[pltpu.VMEM((tm, d), F32)],
        compiler_params=_params("parallel", "arbitrary", "arbitrary"),
        name="moe_dense_res_ln",
    )(x, h, dg, w_gu, w_gu, w_down, g, b)


def _moe_plan(cnt, n_chunks_max, n_tiles):
    nb, n_e = cnt.shape
    cpt = MOE_BLOCK // BF16_ROWS
    chunks = (cnt + BF16_ROWS - 1) // BF16_ROWS
    seg_local = jnp.cumsum(chunks, axis=1) - chunks
    n_chunks = jnp.sum(chunks, axis=1)
    tiles_e = (jnp.sum(chunks, axis=0) + cpt - 1) // cpt
    tile_end = jnp.cumsum(tiles_e)
    seg_global = ((tile_end - tiles_e) * cpt)[None, :] + jnp.cumsum(chunks, axis=0) - chunks
    c = jnp.arange(n_chunks_max, dtype=jnp.int32)
    e_of_c = jnp.sum(c[None, :, None] >= (seg_local + chunks)[:, None, :], axis=2)
    e_of_c = jnp.minimum(e_of_c, n_e - 1).astype(jnp.int32)
    dst = (jnp.take_along_axis(seg_global, e_of_c, axis=1) + c[None, :]
           - jnp.take_along_axis(seg_local, e_of_c, axis=1))
    tile_expert = jnp.sum(jnp.arange(n_tiles, dtype=jnp.int32)[:, None] >= tile_end[None, :], axis=1)
    tile_expert = jnp.minimum(tile_expert, n_e - 1).astype(jnp.int32)
    return (dst.reshape(-1).astype(jnp.int32), n_chunks.astype(jnp.int32), tile_expert,
            tile_end[-1:].astype(jnp.int32))


def _chunk_copy(src_ref, src_chunk, dst_ref, dst_chunk, sem):
    src = src_ref.at[pl.ds(pl.multiple_of(src_chunk * BF16_ROWS, BF16_ROWS), BF16_ROWS), :]
    dst = dst_ref.at[pl.ds(pl.multiple_of(dst_chunk * BF16_ROWS, BF16_ROWS), BF16_ROWS), :]
    return pltpu.make_async_copy(src, dst, sem)


def _dispatch_kernel(dst_ref, nch_ref, x_ref, rows_ref, buf_in_ref, xs_ref, sorted_ref, sem, *, n_chunks_max):
    del buf_in_ref
    b = pl.program_id(0)
    rows = rows_ref[0]
    n_sorted = sorted_ref.shape[0]
    r = lax.broadcasted_iota(jnp.int32, (n_sorted, rows.shape[1]), 0).astype(F32)
    sel = jnp.logical_or(r == rows[0:1, :], r == rows[1:2, :]).astype(BF16)
    sorted_ref[...] = _dot(sel, x_ref[...]).astype(BF16)
    n = nch_ref[b]

    def copy(c):
        return _chunk_copy(sorted_ref, c, xs_ref, dst_ref[b * n_chunks_max + c], sem)

    def start(c, carry):
        copy(c).start()
        return carry

    def wait(c, carry):
        copy(c).wait()
        return carry

    lax.fori_loop(0, n, start, 0)
    lax.fori_loop(0, n, wait, 0)


def _moe_ffn_kernel(te_ref, nt_ref, x_ref, wg_ref, wu_ref, wd_ref, o_ref, acc_ref):
    del te_ref
    i = pl.program_id(0)
    f = pl.program_id(1)

    @pl.when(i < nt_ref[0])
    def _():
        part = _swiglu_part(x_ref[...], wg_ref, wu_ref, wd_ref)

        @pl.when(f == 0)
        def _():
            acc_ref[...] = part

        @pl.when(f > 0)
        def _():
            acc_ref[...] += part

        @pl.when(f == pl.num_programs(1) - 1)
        def _():
            o_ref[...] = acc_ref[...].astype(o_ref.dtype)

    @pl.when(i >= nt_ref[0])
    def _():
        o_ref[...] = jnp.zeros(o_ref.shape, o_ref.dtype)


def _combine_kernel(dst_ref, nch_ref, ys_ref, meta_ref, h_ref, g_ref, b_ref, of_ref, ob_ref, sorted_ref, sem,
                    *, n_chunks_max):
    blk = pl.program_id(0)
    n = nch_ref[blk]

    def copy(c):
        return _chunk_copy(ys_ref, dst_ref[blk * n_chunks_max + c], sorted_ref, c, sem)

    def start(c, carry):
        copy(c).start()
        return carry

    def clear(c, carry):
        sorted_ref[pl.ds(pl.multiple_of(c * BF16_ROWS, BF16_ROWS), BF16_ROWS), :] = jnp.zeros(
            (BF16_ROWS, sorted_ref.shape[1]), sorted_ref.dtype)
        return carry

    def wait(c, carry):
        copy(c).wait()
        return carry

    lax.fori_loop(0, n, start, 0)
    lax.fori_loop(n, n_chunks_max, clear, 0)
    lax.fori_loop(0, n, wait, 0)
    meta = meta_ref[...]
    col = lax.broadcasted_iota(jnp.int32, (meta.shape[0], sorted_ref.shape[0]), 1).astype(F32)
    ys = sorted_ref[...]
    y = (meta[:, 2:3] * _dot((col == meta[:, 0:1]).astype(BF16), ys)
         + meta[:, 3:4] * _dot((col == meta[:, 1:2]).astype(BF16), ys))
    out = _layer_norm(ALPHA * h_ref[...] + y, g_ref[...], b_ref[...])
    of_ref[...] = out
    ob_ref[...] = out.astype(BF16)


def moe_sorted_res_ln(x, h, meta, cnt, w_gu, w_down, g, b, nf=2, lead=()):
    m, d = x.shape
    n_e, ff, _ = w_down.shape[-3:]
    tf = ff // nf
    nb = m // MOE_BLOCK
    n_chunks_max = 2 * MOE_BLOCK // BF16_ROWS + n_e
    n_sorted = n_chunks_max * BF16_ROWS
    cpt = MOE_BLOCK // BF16_ROWS
    n_tiles = -(-nb * n_chunks_max // cpt) + n_e
    dst, n_chunks, tile_expert, n_used = _moe_plan(cnt[:, 0, :n_e].astype(jnp.int32), n_chunks_max, n_tiles)
    rows_t = jnp.swapaxes(meta[:, :SUBLANES].reshape(nb, MOE_BLOCK, SUBLANES), 1, 2)

    any_space = pl.BlockSpec(memory_space=pl.ANY)
    xs = pl.pallas_call(
        functools.partial(_dispatch_kernel, n_chunks_max=n_chunks_max),
        grid_spec=pltpu.PrefetchScalarGridSpec(
            num_scalar_prefetch=2,
            grid=(nb,),
            in_specs=[pl.BlockSpec((MOE_BLOCK, d), lambda i, *_: (i, 0)),
                      pl.BlockSpec((1, SUBLANES, MOE_BLOCK), lambda i, *_: (i, 0, 0)),
                      any_space],
            out_specs=any_space,
            scratch_shapes=[pltpu.VMEM((n_sorted, d), BF16), pltpu.SemaphoreType.DMA(())],
        ),
        out_shape=jax.ShapeDtypeStruct((n_tiles * MOE_BLOCK, d), BF16),
        input_output_aliases={4: 0},
        compiler_params=_params("arbitrary"),
        name="moe_dispatch",
    )(dst, n_chunks, x, rows_t, jnp.zeros((n_tiles * MOE_BLOCK, d), BF16))

    def tile(i, f, te, nt):
        return jnp.maximum(jnp.minimum(i, nt[0] - 1), 0)

    ys = pl.pallas_call(
        _moe_ffn_kernel,
        grid_spec=pltpu.PrefetchScalarGridSpec(
            num_scalar_prefetch=2,
            grid=(n_tiles, nf),
            in_specs=[pl.BlockSpec((MOE_BLOCK, d), lambda i, f, te, nt: (tile(i, f, te, nt), 0)),
                      _wspec(lead, (None, d, tf), lambda i, f, te, nt: (te[tile(i, f, te, nt)], 0, f)),
                      _wspec(lead, (None, d, tf), lambda i, f, te, nt: (te[tile(i, f, te, nt)], 0, nf + f)),
                      _wspec(lead, (None, tf, d), lambda i, f, te, nt: (te[tile(i, f, te, nt)], f, 0))],
            out_specs=pl.BlockSpec((MOE_BLOCK, d), lambda i, f, te, nt: (i, 0)),
            scratch_shapes=[pltpu.VMEM((MOE_BLOCK, d), F32)],
        ),
        out_shape=jax.ShapeDtypeStruct((n_tiles * MOE_BLOCK, d), BF16),
        compiler_params=_params("arbitrary", "arbitrary"),
        name="moe_expert_ffn",
    )(tile_expert, n_used, xs, w_gu, w_gu, w_down)

    row = lambda i, *_: (i, 0)
    fixed = lambda i, *_: (0, 0)
    return pl.pallas_call(
        functools.partial(_combine_kernel, n_chunks_max=n_chunks_max),
        grid_spec=pltpu.PrefetchScalarGridSpec(
            num_scalar_prefetch=2,
            grid=(nb,),
            in_specs=[any_space, pl.BlockSpec((MOE_BLOCK, LANES), row), pl.BlockSpec((MOE_BLOCK, d), row),
                      pl.BlockSpec((1, d), fixed), pl.BlockSpec((1, d), fixed)],
            out_specs=[pl.BlockSpec((MOE_BLOCK, d), row), pl.BlockSpec((MOE_BLOCK, d), row)],
            scratch_shapes=[pltpu.VMEM((n_sorted, d), BF16), pltpu.SemaphoreType.DMA(())],
        ),
        out_shape=[jax.ShapeDtypeStruct((m, d), F32), jax.ShapeDtypeStruct((m, d), BF16)],
        compiler_params=_params("arbitrary"),
        name="moe_combine_res_ln",
    )(dst, n_chunks, ys, meta, h, g, b)


def _log_sigmoid(x):
    return jnp.minimum(x, 0.0) - jnp.log(1.0 + jnp.exp(-jnp.abs(x)))


def _mlstm_kernel(q_ref, k_ref, v_ref, o_ref, gc_ref, gr_ref, bc_ref, br_ref, gh_ref, c0_ref, n0_ref, m0_ref,
                  hg_ref, c_ref, n_ref, m_ref, *, chunk, t_valid, nh, dh):
    ci = pl.program_id(1)

    @pl.when(ci == 0)
    def _():
        c_ref[...] = c0_ref[...]
        n_ref[...] = n0_ref[...]
        m_ref[...] = m0_ref[...]

    gc = gc_ref[0] + bc_ref[...]
    gr = gr_ref[0] + br_ref[...]
    li_c, lf_c = gc[:, :nh], _log_sigmoid(gc[:, nh:])
    li_r, lf_r = gr[:nh, :], _log_sigmoid(gr[nh:, :])
    if t_valid < chunk:
        tok_c = lax.broadcasted_iota(jnp.int32, li_c.shape, 0) < t_valid
        tok_r = lax.broadcasted_iota(jnp.int32, li_r.shape, 1) < t_valid
        li_c, lf_c = jnp.where(tok_c, li_c, NEG), jnp.where(tok_c, lf_c, 0.0)
        li_r, lf_r = jnp.where(tok_r, li_r, NEG), jnp.where(tok_r, lf_r, 0.0)
    row = lax.broadcasted_iota(jnp.int32, (chunk, chunk), 0)
    col = lax.broadcasted_iota(jnp.int32, (chunk, chunk), 1)
    causal = col <= row
    hi = lax.Precision.HIGHEST
    b_c = jnp.dot(causal.astype(F32), lf_c, preferred_element_type=F32, precision=hi)
    b_r = jnp.dot(lf_r, (row <= col).astype(F32), preferred_element_type=F32, precision=hi)

    mm = q_ref.dtype
    staged = []
    for h in range(nh):
        hs = slice(h * dh, (h + 1) * dh)
        bcol, icol = b_c[:, h:h + 1], li_c[:, h:h + 1]
        brow, irow = b_r[h:h + 1, :], li_r[h:h + 1, :]
        m_prev = m_ref[0, h, :, 0:1]
        c_prev = c_ref[0, h]
        q = q_ref[:, hs]
        kf = k_ref[:, hs].astype(F32) * (dh ** -0.5)
        dmat = jnp.where(causal, bcol - brow + irow, NEG)
        inter = bcol + m_prev
        mt = jnp.maximum(inter, jnp.max(dmat, axis=1, keepdims=True))
        w_inter = jnp.exp(inter - mt)
        qk = _dot_nt(q, kf.astype(mm))
        qc = _dot(q, c_prev.astype(mm))
        qn = jnp.sum(q.astype(F32) * n_ref[0, h], axis=1, keepdims=True)
        staged.append((hs, bcol, icol, m_prev, c_prev, q, kf, dmat, mt, w_inter, qk, qc, qn))

    for h, (hs, bcol, icol, m_prev, c_prev, q, kf, dmat, mt, w_inter, qk, qc, qn) in enumerate(staged):
        n_prev = n_ref[0, h]
        v = v_ref[:, hs]
        b_last = bcol[chunk - 1:chunk, :]
        gdec = b_last - bcol + icol
        m_new = jnp.maximum(b_last + m_prev, jnp.max(gdec, axis=0, keepdims=True))
        decay = jnp.exp(b_last + m_prev - m_new)
        kw = kf * jnp.exp(gdec - m_new)
        c_ref[0, h] = decay * c_prev + _dot_tn(kw.astype(mm), v)
        n_ref[0, h] = decay * n_prev + jnp.sum(kw, axis=0, keepdims=True)
        m_ref[0, h] = jnp.broadcast_to(m_new, (1, LANES))

    readout = []
    for h, (hs, bcol, icol, m_prev, c_prev, q, kf, dmat, mt, w_inter, qk, qc, qn) in enumerate(staged):
        s = qk * jnp.exp(dmat - mt)
        num = w_inter * qc + _dot(s.astype(mm), v_ref[:, hs])
        den = w_inter * qn + jnp.sum(s, axis=1, keepdims=True)
        readout.append((num, den))

    for h, (num, den) in enumerate(readout):
        hs, mt = staged[h][0], staged[h][8]
        hh = num * (1.0 / jnp.maximum(jnp.abs(den), jnp.exp(-mt)))
        mu = jnp.mean(hh, axis=1, keepdims=True)
        hc = hh - mu
        var = jnp.mean(hc * hc, axis=1, keepdims=True)
        hn = hc * lax.rsqrt(var + LN_EPS) * gh_ref[h:h + 1, :]
        hg_ref[:, hs] = (hn * _sigmoid(o_ref[:, hs])).astype(hg_ref.dtype)


def mlstm(qkv, o, gates, b_gate, g_head, c0, n0, m0, t_valid, chunk):
    bsz, t_len, d3 = qkv.shape
    d = d3 // 3
    nh = g_head.shape[0]
    dh = d // nh
    nc = t_len // chunk
    gates_t = jnp.swapaxes(gates, 1, 2)
    seq = lambda b, c: (b, 0, 0, 0)
    fixed = lambda b, c: (0, 0)
    kern = functools.partial(_mlstm_kernel, chunk=chunk, t_valid=t_valid, nh=nh, dh=dh)
    hg, c, n, m = pl.pallas_call(
        kern,
        grid=(bsz, nc),
        in_specs=[pl.BlockSpec((None, chunk, d), lambda b, c: (b, c, 0)),
                  pl.BlockSpec((None, chunk, d), lambda b, c: (b, c, 1)),
                  pl.BlockSpec((None, chunk, d), lambda b, c: (b, c, 2)),
                  pl.BlockSpec((None, chunk, d), lambda b, c: (b, c, 0)),
                  pl.BlockSpec((1, chunk, 2 * nh), lambda b, c: (b, c, 0)),
                  pl.BlockSpec((1, 2 * nh, chunk), lambda b, c: (b, 0, c)),
                  pl.BlockSpec((1, 2 * nh), fixed), pl.BlockSpec((2 * nh, 1), fixed),
                  pl.BlockSpec((nh, dh), fixed),
                  pl.BlockSpec((1, nh, dh, dh), seq), pl.BlockSpec((1, nh, 1, dh), seq),
                  pl.BlockSpec((1, nh, 1, LANES), seq)],
        out_specs=[pl.BlockSpec((None, chunk, d), lambda b, c: (b, c, 0)),
                   pl.BlockSpec((1, nh, dh, dh), seq), pl.BlockSpec((1, nh, 1, dh), seq),
                   pl.BlockSpec((1, nh, 1, LANES), seq)],
        out_shape=[jax.ShapeDtypeStruct((bsz, t_len, d), qkv.dtype),
                   jax.ShapeDtypeStruct((bsz, nh, dh, dh), F32),
                   jax.ShapeDtypeStruct((bsz, nh, 1, dh), F32),
                   jax.ShapeDtypeStruct((bsz, nh, 1, LANES), F32)],
        compiler_params=_params("parallel", "arbitrary"),
        name="mlstm",
    )(qkv, qkv, qkv, o, gates, gates_t, b_gate.reshape(1, 2 * nh), b_gate.reshape(2 * nh, 1), g_head,
      c0, n0.reshape(bsz, nh, 1, dh), jnp.broadcast_to(m0[:, :, None, None], (bsz, nh, 1, LANES)))
    return hg, c, n.reshape(bsz, nh, dh), m[:, :, 0, 0]


def _lambda_value(lam_ref, lam_init):
    lp = lam_ref[...]
    a = jnp.sum(lp[0:1, :] * lp[1:2, :], axis=1, keepdims=True)
    b = jnp.sum(lp[2:3, :] * lp[3:4, :], axis=1, keepdims=True)
    return jnp.exp(a) - jnp.exp(b) + lam_init


def _split_queries(q, dh):
    lane = lax.broadcasted_iota(jnp.int32, q.shape, 1)
    zero = jnp.zeros_like(q)
    return jnp.concatenate([jnp.where(lane < dh, q, zero), jnp.where(lane >= dh, q, zero)], axis=0)


def _flash_step(qi, q_ref, k_ref, vt_ref, lam_ref, g_ref, o_ref, qs_ref, s_ref, m_ref, l_ref, acc_ref,
                *, blk, dh, lam_init, interleave=()):
    qs_ref[...] = _split_queries(q_ref[...], dh)
    m_ref[...] = jnp.full(m_ref.shape, NEG, F32)
    l_ref[...] = jnp.zeros(l_ref.shape, F32)
    acc_ref[...] = jnp.zeros(acc_ref.shape, F32)
    sub = 2 * blk // ATTN_QSPLIT

    def scores(ki, c):
        k = k_ref[pl.ds(pl.multiple_of(ki * blk, blk), blk), :]
        return _dot_nt(k, qs_ref[c * sub:(c + 1) * sub, :])

    def update(ki, c, s, diagonal):
        cs = slice(c * sub, (c + 1) * sub)
        vt = vt_ref[:, pl.ds(pl.multiple_of(ki * blk, blk), blk)]
        if diagonal:
            key = lax.broadcasted_iota(jnp.int32, s.shape, 0)
            qry = (lax.broadcasted_iota(jnp.int32, s.shape, 1) + c * sub) % blk
            s = jnp.where(key <= qry, s, NEG)
        m_prev = m_ref[:, cs]
        m_new = jnp.maximum(m_prev, jnp.max(s, axis=0, keepdims=True))
        alpha = jnp.exp2(m_prev - m_new)
        p = jnp.exp2(s - m_new)
        l_ref[:, cs] = alpha * l_ref[:, cs] + jnp.sum(p, axis=0, keepdims=True)
        acc_ref[:, cs] = alpha * acc_ref[:, cs] + _dot(vt, p.astype(BF16))
        m_ref[:, cs] = m_new

    def block(ki, diagonal):
        pending = [s_ref[i] for i in range(ATTN_LOOKAHEAD)]
        for c in range(ATTN_QSPLIT):
            nxt = c + ATTN_LOOKAHEAD
            if nxt < ATTN_QSPLIT:
                pending.append(scores(ki, nxt))
            elif not diagonal:
                pending.append(scores(ki + 1, nxt - ATTN_QSPLIT))
            if diagonal and c < len(interleave):
                interleave[c]()
            update(ki, c, pending.pop(0), diagonal)
        if diagonal:
            for extra in interleave[ATTN_QSPLIT:]:
                extra()
        for i, s in enumerate(pending):
            s_ref[i] = s

    def body(ki, carry):
        block(ki, False)
        return carry

    for i in range(ATTN_LOOKAHEAD):
        s_ref[i] = scores(0, i)
    lax.fori_loop(0, qi, body, 0)
    block(qi, True)
    lam = _lambda_value(lam_ref, lam_init)
    acc = acc_ref[...]
    r = 1.0 / l_ref[...]
    o = acc[:, :blk] * r[:, :blk] - lam * (acc[:, blk:] * r[:, blk:])
    ms = jnp.mean(o * o, axis=0, keepdims=True)
    o = o * lax.rsqrt(ms + LN_EPS) * (g_ref[...] * (1.0 - lam_init))
    o_ref[...] = o.T.astype(o_ref.dtype)


def _decode_stages(j, n_steps, q_ref, k_refs, v_refs, kn_ref, vn_ref, lam_ref, g_ref, o_ref, qs_ref, m_ref, l_ref,
                   acc_ref, s_ref, p_ref, alpha_ref, *, nh, dh, lam_init):
    page = k_refs[0].shape[0]
    rows = page * nh

    def begin():
        @pl.when(j == 0)
        def _():
            qs = _split_queries(q_ref[0], dh) * (dh ** -0.5)
            qs_ref[...] = qs.astype(BF16)
            kn = jnp.concatenate([kn_ref[0, 0], kn_ref[0, 0]], axis=0)
            m_ref[...] = jnp.sum(qs * kn, axis=1, keepdims=True)
            l_ref[...] = jnp.ones(l_ref.shape, F32)
            acc_ref[...] = jnp.concatenate([vn_ref[0, 0], vn_ref[0, 0]], axis=0)

    def scores():
        qs = qs_ref[...]
        s = jnp.concatenate([_dot_nt(qs, k_ref[...].reshape(rows, 2 * dh).astype(BF16)) for k_ref in k_refs], axis=1)
        row = lax.broadcasted_iota(jnp.int32, s.shape, 0) % nh
        col = lax.broadcasted_iota(jnp.int32, s.shape, 1) % nh
        s_ref[...] = jnp.where(row == col, s, NEG)

    def softmax():
        s = s_ref[...]
        m_prev = m_ref[...]
        m_new = jnp.maximum(m_prev, jnp.max(s, axis=1, keepdims=True))
        alpha = jnp.exp(m_prev - m_new)
        p = jnp.exp(s - m_new).astype(BF16)
        l_ref[...] = alpha * l_ref[...] + jnp.sum(p.astype(F32), axis=1, keepdims=True)
        m_ref[...] = m_new
        alpha_ref[...] = alpha
        p_ref[...] = p

    def accumulate():
        pv = _dot(p_ref[:, :rows], v_refs[0][...].reshape(rows, 2 * dh).astype(BF16))
        for i in range(1, len(v_refs)):
            pv += _dot(p_ref[:, i * rows:(i + 1) * rows], v_refs[i][...].reshape(rows, 2 * dh).astype(BF16))
        acc_ref[...] = alpha_ref[...] * acc_ref[...] + pv

    def finish():
        @pl.when(j == n_steps - 1)
        def _():
            lam = _lambda_value(lam_ref, lam_init)
            acc = acc_ref[...]
            r = 1.0 / l_ref[...]
            o = acc[:nh] * r[:nh] - lam * (acc[nh:] * r[nh:])
            ms = jnp.mean(o * o, axis=1, keepdims=True)
            o_ref[0] = o * lax.rsqrt(ms + LN_EPS) * g_ref[...] * (1.0 - lam_init)

    return begin, scores, softmax, accumulate, finish


def _attention_kernel(pt_ref, q_ref, k_ref, vt_ref, lam_ref, gcol_ref, qd_ref, ck_ref, cv_ref, kn_ref, vn_ref,
                      grow_ref, o_ref, od_ref, qs_ref, s_ref, m_ref, l_ref, acc_ref, dqs_ref, dm_ref, dl_ref,
                      dacc_ref, ds_ref, dp_ref, dalpha_ref, kbuf_ref, vbuf_ref, sem,
                      *, blk, dh, nh, lam_init, pps, steps_per_seq, n_decode_steps, n_grid_steps):
    step = (pl.program_id(0) * pl.num_programs(1) + pl.program_id(1)) * pl.num_programs(2) + pl.program_id(2)
    slot = step % 2

    def page_copies(t, into):
        seq, first = t // steps_per_seq, (t % steps_per_seq) * pps
        for p in range(pps):
            phys = pt_ref[seq, first + p]
            yield pltpu.make_async_copy(ck_ref.at[phys], kbuf_ref.at[into, p], sem.at[into])
            yield pltpu.make_async_copy(cv_ref.at[phys], vbuf_ref.at[into, p], sem.at[into])

    @pl.when(step == 0)
    def _():
        for copy in page_copies(step, slot):
            copy.start()

    @pl.when(step + 1 < n_decode_steps)
    def _():
        for copy in page_copies(step + 1, 1 - slot):
            copy.start()

    @pl.when(step < n_decode_steps)
    def _():
        for copy in page_copies(step, slot):
            copy.wait()

    kd_refs = [kbuf_ref.at[slot, p] for p in range(pps)]
    vd_refs = [vbuf_ref.at[slot, p] for p in range(pps)]
    stages = _decode_stages(step % steps_per_seq, steps_per_seq, qd_ref, kd_refs, vd_refs, kn_ref, vn_ref, lam_ref,
                            grow_ref, od_ref, dqs_ref, dm_ref, dl_ref, dacc_ref, ds_ref, dp_ref, dalpha_ref,
                            nh=nh, dh=dh, lam_init=lam_init)
    if n_decode_steps != n_grid_steps:
        stages = [functools.partial(pl.when(step < n_decode_steps), f) for f in stages]
    begin, scores, softmax, accumulate, finish = stages
    begin()
    _flash_step(pl.program_id(2), q_ref, k_ref, vt_ref, lam_ref, gcol_ref, o_ref, qs_ref, s_ref, m_ref, l_ref,
                acc_ref, blk=blk, dh=dh, lam_init=lam_init, interleave=(scores, softmax, accumulate))
    finish()


def diff_attention(q, k, vt, q_dec, k_new, v_new, cache_k, cache_v, page_table, lam_p, g_subln, lam_init,
                   blk=ATTN_BLOCK):
    bsz, t_len, d = q.shape
    n_seq, nh, hw = q_dec.shape
    dh = hw // 2
    blk = min(blk, t_len)
    nq = t_len // blk
    n_steps = bsz * nh * nq
    n_pages = page_table.shape[1]
    page = cache_k.shape[1]
    pps = min(p for p in range(1, n_pages + 1) if n_pages % p == 0 and n_seq * (n_pages // p) <= n_steps)
    steps_per_seq = n_pages // pps
    n_decode_steps = n_seq * steps_per_seq

    def decode_pos(b, h, i):
        step = jnp.minimum((b * nh + h) * nq + i, n_decode_steps - 1)
        return step // steps_per_seq, step % steps_per_seq

    def seq3(b, h, i, pt):
        return (decode_pos(b, h, i)[0], 0, 0)

    def seq4(b, h, i, pt):
        return (decode_pos(b, h, i)[0], 0, 0, 0)

    in_hbm = pl.BlockSpec(memory_space=pl.ANY)
    fixed = lambda b, h, i, pt: (0, 0)
    grid_spec = pltpu.PrefetchScalarGridSpec(
        num_scalar_prefetch=1,
        grid=(bsz, nh, nq),
        in_specs=([pl.BlockSpec((None, blk, hw), lambda b, h, i, pt: (b, i, h)),
                   pl.BlockSpec((None, t_len, hw), lambda b, h, i, pt: (b, 0, h)),
                   pl.BlockSpec((None, hw, t_len), lambda b, h, i, pt: (b, h, 0)),
                   pl.BlockSpec(lam_p.shape, fixed), pl.BlockSpec((hw, 1), fixed),
                   pl.BlockSpec((1, nh, hw), seq3), in_hbm, in_hbm,
                   pl.BlockSpec((1, 1, nh, hw), seq4), pl.BlockSpec((1, 1, nh, hw), seq4),
                   pl.BlockSpec((1, hw), fixed)]),
        out_specs=[pl.BlockSpec((None, blk, hw), lambda b, h, i, pt: (b, i, h)),
                   pl.BlockSpec((1, nh, hw), seq3)],
        scratch_shapes=[pltpu.VMEM((2 * blk, hw), BF16),
                        pltpu.VMEM((ATTN_LOOKAHEAD, blk, 2 * blk // ATTN_QSPLIT), F32),
                        pltpu.VMEM((1, 2 * blk), F32), pltpu.VMEM((1, 2 * blk), F32), pltpu.VMEM((hw, 2 * blk), F32),
                        pltpu.VMEM((2 * nh, hw), BF16), pltpu.VMEM((2 * nh, 1), F32),
                        pltpu.VMEM((2 * nh, 1), F32), pltpu.VMEM((2 * nh, hw), F32),
                        pltpu.VMEM((2 * nh, pps * page * nh), F32), pltpu.VMEM((2 * nh, pps * page * nh), BF16),
                        pltpu.VMEM((2 * nh, 1), F32),
                        pltpu.VMEM((2, pps, page, nh, hw), F32), pltpu.VMEM((2, pps, page, nh, hw), F32),
                        pltpu.SemaphoreType.DMA((2,))],
    )
    return pl.pallas_call(
        functools.partial(_attention_kernel, blk=blk, dh=dh, nh=nh, lam_init=lam_init, pps=pps,
                          steps_per_seq=steps_per_seq, n_decode_steps=n_decode_steps, n_grid_steps=n_steps),
        grid_spec=grid_spec,
        out_shape=[jax.ShapeDtypeStruct((bsz, t_len, d), BF16), jax.ShapeDtypeStruct((n_seq, nh, hw), F32)],
        compiler_params=_params("arbitrary", "arbitrary", "arbitrary"),
        name="diff_attention",
    )(page_table, q, k, vt, lam_p, g_subln.reshape(hw, 1), q_dec, cache_k, cache_v, k_new, v_new,
      g_subln.reshape(1, hw))


def _trunk(x, state, decode, wts):
    bsz, t_len, d = x.shape
    m_rows = bsz * t_len
    n_a = wts["w_in"].shape[0]
    hw = wts["g_subln"].shape[-1]
    n_heads = d // hw
    act = wts["w_in"].dtype
    pick = (lambda h32, hbf: hbf) if act == BF16 else (lambda h32, hbf: h32)
    h32 = x.reshape(m_rows, d)
    ha = h32.astype(act)
    cs, ns, ms = [], [], []
    k32 = v32 = kbf = vt = None
    for i in range(DEPTH):
        ln_g, ln_b = wts["ln_g"][i], wts["ln_b"][i]
        if i < n_a:
            (qkv,) = matmul(ha, wts["w_in"], [act], lead=(i,), n=3 * d)
            (og,) = matmul(ha, wts["w_in"], [F32], lead=(i,), col0=3, n=d)
            (gates,) = matmul(ha, wts["w_gates"], [F32], lead=(i,))
            chunk = MLSTM_CHUNK if t_len % MLSTM_CHUNK == 0 else BF16_ROWS
            t_pad = -(-t_len // chunk) * chunk
            gates = gates[:, :2 * A_HEADS].reshape(bsz, t_len, 2 * A_HEADS)
            og = og.reshape(bsz, t_len, d)
            qkv = qkv.reshape(bsz, t_len, 3 * d)
            if t_pad != t_len:
                pad = ((0, 0), (0, t_pad - t_len), (0, 0))
                qkv, og, gates = jnp.pad(qkv, pad), jnp.pad(og, pad), jnp.pad(gates, pad)
            if state is None:
                dh = d // A_HEADS
                c0 = jnp.zeros((bsz, A_HEADS, dh, dh), F32)
                n0 = jnp.zeros((bsz, A_HEADS, dh), F32)
                m0 = jnp.zeros((bsz, A_HEADS), F32)
            else:
                c0, n0, m0 = state[0][i], state[1][i], state[2][i]
            hg, c, n, m = mlstm(qkv, og, gates, wts["b_gate"][i], wts["g_head"][i], c0, n0, m0, t_len, chunk)
            cs.append(c)
            ns.append(n)
            ms.append(m)
            mix_in = hg[:, :t_len].reshape(m_rows, d)
            w_mix, mix_lead = wts["w_out"], (i,)
        else:
            j = i - n_a
            if j == 0:
                if not decode:
                    k32, kbf = matmul(ha, wts["w_kv"], [F32, BF16], n=d)
                    (v32,) = matmul(ha, wts["w_kv"], [F32], col0=1, n=d)
                    vt = matmul_t(ha.reshape(bsz, t_len, d), wts["w_v_t"])
                else:
                    (k32,) = matmul(ha, wts["w_kv"], [F32], n=d)
                    (v32,) = matmul(ha, wts["w_kv"], [F32], col0=1, n=d)
            if not decode:
                (q,) = matmul(ha, wts["w_q"], [BF16], lead=(j,), scale=math.log2(math.e) * (hw // 2) ** -0.5)
                o = yield {"q": q.reshape(bsz, t_len, d), "k": kbf.reshape(bsz, t_len, d), "vt": vt}
                mix_in = o.reshape(m_rows, d)
            else:
                (q,) = matmul(ha, wts["w_q"], [F32], lead=(j,))
                o = yield {"q": q.reshape(bsz, n_heads, hw), "k_new": k32.reshape(bsz, 1, n_heads, hw),
                           "v_new": v32.reshape(bsz, 1, n_heads, hw)}
                mix_in = o.reshape(m_rows, d).astype(act)
            w_mix, mix_lead = wts["w_o"], (j,)
        h32, hbf = matmul_res_ln(mix_in, w_mix, h32, ln_g[0:1], ln_b[0:1], lead=mix_lead)
        ha = pick(h32, hbf)
        if i % 2 == 0:
            h32, hbf = ffn_res_ln(ha, h32, wts["w_ffn_gu"], wts["w_ffn_down"], ln_g[1:2], ln_b[1:2], lead=(i // 2,))
        else:
            dg, meta, cnt = router(h32, wts["w_router"][i // 2])
            w_gu, w_down, lead = wts["w_moe_gu"], wts["w_moe_down"], (i // 2,)
            if m_rows >= MOE_MIN_ROWS and m_rows % MOE_BLOCK == 0:
                h32, hbf = moe_sorted_res_ln(hbf, h32, meta, cnt, w_gu, w_down, ln_g[1:2], ln_b[1:2], lead=lead)
            else:
                h32, hbf = moe_dense_res_ln(hbf, h32, dg, w_gu, w_down, ln_g[1:2], ln_b[1:2], lead=lead)
        ha = pick(h32, hbf)
    y = h32.reshape(bsz, t_len, d)
    k_rows = k32.reshape(bsz, t_len, n_heads, hw)
    v_rows = v32.reshape(bsz, t_len, n_heads, hw)
    return y, jnp.stack(cs), jnp.stack(ns), jnp.stack(ms), k_rows, v_rows


def _resume(gen, value):
    try:
        return gen.send(value), None
    except StopIteration as done:
        return None, done.value


def kernel(x_prompt, x_sample, state_c, state_n, state_m, cache_k, cache_v, page_table, ln_g, ln_b, w_in_a, b_gate_a, g_head_a, w_out_a, w_kv, w_q_b, lam_b, g_subln_b, w_o_b, w_ffn_gu, w_ffn_down, w_router, w_moe_gu, w_moe_down):
    d = x_prompt.shape[-1]
    n_gate = w_in_a.shape[-1] - 4 * d
    w_gates = jnp.pad(w_in_a[:, :, 4 * d:], ((0, 0), (0, 0), (0, LANES - n_gate)))

    moe_bf16 = {"w_moe_gu": w_moe_gu.astype(BF16), "w_moe_down": w_moe_down.astype(BF16)}

    def weights(cast):
        return {
            "ln_g": ln_g, "ln_b": ln_b, "b_gate": b_gate_a, "g_head": g_head_a, "w_router": w_router,
            "g_subln": g_subln_b,
            "w_in": cast(w_in_a), "w_gates": cast(w_gates), "w_out": cast(w_out_a), "w_kv": cast(w_kv),
            "w_v_t": cast(w_kv[:, d:].T), "w_q": cast(w_q_b), "w_o": cast(w_o_b), "w_ffn_gu": cast(w_ffn_gu),
            "w_ffn_down": cast(w_ffn_down), **moe_bf16,
        }

    prompt = _trunk(x_prompt, None, False, weights(lambda a: a.astype(BF16)))
    sample = _trunk(x_sample, (state_c, state_n, state_m), True, weights(lambda a: a))
    n_a = w_in_a.shape[0]
    req_p, req_s = next(prompt), next(sample)
    out_p = out_s = None
    for j in range(w_q_b.shape[0]):
        o_p, o_s = diff_attention(req_p["q"], req_p["k"], req_p["vt"], req_s["q"], req_s["k_new"], req_s["v_new"],
                                  cache_k, cache_v, page_table, lam_b[j], g_subln_b[j], _lambda_init(n_a + j))
        req_p, out_p = _resume(prompt, o_p)
        req_s, out_s = _resume(sample, o_s)
    y_p, c_p, n_p, m_p, k_p, v_p = out_p
    y_s, c_s, n_s, m_s, k_s, v_s = out_s
    return (y_p, y_s, c_p, n_p, m_p, k_p, v_p, c_s, n_s, m_s, k_s, v_s)
```

```python
import functools
import math

import jax
import jax.numpy as jnp
from jax import lax
from jax.experimental import pallas as pl
from jax.experimental.pallas import tpu as pltpu

F32 = jnp.float32
BF16 = jnp.bfloat16

DEPTH = 4
ALPHA = (2 * DEPTH) ** 0.25
LN_EPS = 1e-5
NEG = -1e30
LANES = 128
SUBLANES = 8
BF16_ROWS = 16
VMEM_LIMIT = 56 * 1024 * 1024

A_HEADS = 4
MLSTM_CHUNK = 256
ATTN_BLOCK = 512
ATTN_QSPLIT = 2
ATTN_LOOKAHEAD = 2
MOE_BLOCK = 512
MOE_MIN_ROWS = 1024


def _lambda_init(layer_idx):
    return 0.8 - 0.6 * math.exp(-0.3 * layer_idx)


def _params(*sem):
    return pltpu.CompilerParams(dimension_semantics=sem, vmem_limit_bytes=VMEM_LIMIT)


def _contract(a, b, dims):
    precision = lax.Precision.HIGHEST if a.dtype == F32 and b.dtype == F32 else None
    return lax.dot_general(a, b, (dims, ((), ())), preferred_element_type=F32, precision=precision)


def _dot(a, b):
    return _contract(a, b, ((1,), (0,)))


def _dot_nt(a, b):
    return _contract(a, b, ((1,), (1,)))


def _dot_tn(a, b):
    return _contract(a, b, ((0,), (0,)))


def _layer_norm(y, g, b):
    mu = jnp.mean(y, axis=-1, keepdims=True)
    yc = y - mu
    var = jnp.mean(yc * yc, axis=-1, keepdims=True)
    return yc * lax.rsqrt(var + LN_EPS) * g + b


def _sigmoid(x):
    return 0.5 * jnp.tanh(0.5 * x) + 0.5


def _row_tile(m, want):
    return want if m % want == 0 else m


def _wspec(lead, block, index):
    return pl.BlockSpec((None,) * len(lead) + block, lambda *a: tuple(lead) + index(*a))


def _mm_kernel(x_ref, w_ref, *o_refs, scale):
    acc = _dot(x_ref[...], w_ref[...])
    if scale != 1.0:
        acc = acc * scale
    for o_ref in o_refs:
        o_ref[...] = acc.astype(o_ref.dtype).reshape(o_ref.shape)


def matmul(x, w, out_dtypes, tm=1024, tn=1024, scale=1.0, lead=(), col0=0, n=None, heads=None):
    m, k = x.shape
    n = w.shape[-1] if n is None else n
    tm = _row_tile(m, tm)
    tn = _row_tile(n, tn)
    out_specs = [pl.BlockSpec((tm, tn), lambda i, j: (i, j)) for _ in out_dtypes]
    out_shape = [jax.ShapeDtypeStruct((m, n), d) for d in out_dtypes]
    if heads is not None:
        assert tn == n
        out_specs[0] = pl.BlockSpec((tm, heads, n // heads), lambda i, j: (i, 0, 0))
        out_shape[0] = jax.ShapeDtypeStruct((m, heads, n // heads), out_dtypes[0])
    return pl.pallas_call(
        functools.partial(_mm_kernel, scale=scale),
        grid=(m // tm, n // tn),
        in_specs=[pl.BlockSpec((tm, k), lambda i, j: (i, 0)),
                  _wspec(lead, (k, tn), lambda i, j: (0, col0 + j))],
        out_specs=out_specs,
        out_shape=out_shape,
        compiler_params=_params("parallel", "parallel"),
        name="matmul",
    )(x, w)


def _mm_t_kernel(x_ref, wt_ref, o_ref):
    o_ref[...] = _dot_nt(wt_ref[...], x_ref[...]).astype(o_ref.dtype)


def matmul_t(x, wt, tm=1024, tn=1024):
    bsz, t_len, k = x.shape
    n = wt.shape[0]
    tm = _row_tile(t_len, tm)
    tn = _row_tile(n, tn)
    return pl.pallas_call(
        _mm_t_kernel,
        grid=(bsz, t_len // tm, n // tn),
        in_specs=[pl.BlockSpec((None, tm, k), lambda b, i, j: (b, i, 0)),
                  pl.BlockSpec((tn, k), lambda b, i, j: (j, 0))],
        out_specs=pl.BlockSpec((None, tn, tm), lambda b, i, j: (b, j, i)),
        out_shape=jax.ShapeDtypeStruct((bsz, n, t_len), BF16),
        compiler_params=_params("parallel", "parallel", "parallel"),
        name="matmul_t",
    )(x, wt)


def _mm_res_ln_kernel(x_ref, w_ref, h_ref, g_ref, b_ref, of_ref, ob_ref):
    y = ALPHA * h_ref[...] + _dot(x_ref[...], w_ref[...])
    out = _layer_norm(y, g_ref[...], b_ref[...])
    of_ref[...] = out
    ob_ref[...] = out.astype(BF16)


def matmul_res_ln(x, w, h, g, b, tm=1024, lead=()):
    m, k = x.shape
    d = w.shape[-1]
    tm = _row_tile(m, tm)
    row = lambda i: (i, 0)
    fixed = lambda i: (0, 0)
    return pl.pallas_call(
        _mm_res_ln_kernel,
        grid=(m // tm,),
        in_specs=[pl.BlockSpec((tm, k), row), _wspec(lead, (k, d), fixed), pl.BlockSpec((tm, d), row),
                  pl.BlockSpec((1, d), fixed), pl.BlockSpec((1, d), fixed)],
        out_specs=[pl.BlockSpec((tm, d), row), pl.BlockSpec((tm, d), row)],
        out_shape=[jax.ShapeDtypeStruct((m, d), F32), jax.ShapeDtypeStruct((m, d), BF16)],
        compiler_params=_params("parallel"),
        name="matmul_res_ln",
    )(x, w, h, g, b)


def _swiglu_part(x, wg_ref, wu_ref, wd_ref):
    gate = _dot(x, wg_ref[...])
    up = _dot(x, wu_ref[...])
    act = (gate * _sigmoid(gate) * up).astype(wd_ref.dtype)
    return _dot(act, wd_ref[...])


def _ffn_kernel(x_ref, h_ref, wg_ref, wu_ref, wd_ref, g_ref, b_ref, of_ref, ob_ref, acc_ref):
    f = pl.program_id(1)
    part = _swiglu_part(x_ref[...], wg_ref, wu_ref, wd_ref)

    @pl.when(f == 0)
    def _():
        acc_ref[...] = part

    @pl.when(f > 0)
    def _():
        acc_ref[...] += part

    @pl.when(f == pl.num_programs(1) - 1)
    def _():
        out = _layer_norm(ALPHA * h_ref[...] + acc_ref[...], g_ref[...], b_ref[...])
        of_ref[...] = out
        ob_ref[...] = out.astype(BF16)


def ffn_res_ln(x, h, w_gu, w_down, g, b, tm=512, nf=2, lead=()):
    m, d = x.shape
    ff = w_down.shape[-2]
    tf = ff // nf
    tm = _row_tile(m, tm)
    row = lambda i, f: (i, 0)
    fixed = lambda i, f: (0, 0)
    return pl.pallas_call(
        _ffn_kernel,
        grid=(m // tm, nf),
        in_specs=[pl.BlockSpec((tm, d), row), pl.BlockSpec((tm, d), row),
                  _wspec(lead, (d, tf), lambda i, f: (0, f)),
                  _wspec(lead, (d, tf), lambda i, f: (0, nf + f)),
                  _wspec(lead, (tf, d), lambda i, f: (f, 0)),
                  pl.BlockSpec((1, d), fixed), pl.BlockSpec((1, d), fixed)],
        out_specs=[pl.BlockSpec((tm, d), row), pl.BlockSpec((tm, d), row)],
        out_shape=[jax.ShapeDtypeStruct((m, d), F32), jax.ShapeDtypeStruct((m, d), BF16)],
        scratch_shapes=[pltpu.VMEM((tm, d), F32)],
        compiler_params=_params("parallel", "arbitrary"),
        name="ffn_res_ln",
    )(x, h, w_gu, w_gu, w_down, g, b)


def _router_kernel(x_ref, w_ref, dg_ref, meta_ref, cnt_ref, *, n_experts):
    logits = jnp.dot(x_ref[...], w_ref[...], preferred_element_type=F32, precision=lax.Precision.HIGHEST)
    lane = lax.broadcasted_iota(jnp.int32, logits.shape, 1)
    lg = jnp.where(lane < n_experts, logits, -jnp.inf)
    v1 = jnp.max(lg, axis=1, keepdims=True)
    i1 = jnp.min(jnp.where(lg == v1, lane, LANES), axis=1, keepdims=True)
    lg2 = jnp.where(lane == i1, -jnp.inf, lg)
    v2 = jnp.max(lg2, axis=1, keepdims=True)
    i2 = jnp.min(jnp.where(lg2 == v2, lane, LANES), axis=1, keepdims=True)
    e = jnp.exp(v2 - v1)
    g1 = 1.0 / (1.0 + e)
    g2 = e / (1.0 + e)
    pick1, pick2 = lane == i1, lane == i2
    dg_ref[...] = jnp.where(pick1, g1, 0.0) + jnp.where(pick2, g2, 0.0)

    rows = logits.shape[0]
    onehot = jnp.logical_or(pick1, pick2).astype(BF16)
    r = lax.broadcasted_iota(jnp.int32, (rows, rows), 0)
    c = lax.broadcasted_iota(jnp.int32, (rows, rows), 1)
    rank = _dot((c < r).astype(BF16), onehot)
    cnt = jnp.sum(onehot.astype(F32), axis=0, keepdims=True)
    padded = jnp.floor((cnt + (BF16_ROWS - 1)) * (1.0 / BF16_ROWS)) * BF16_ROWS
    er = lax.broadcasted_iota(jnp.int32, (LANES, LANES), 0)
    ec = lax.broadcasted_iota(jnp.int32, (LANES, LANES), 1)
    padded_rows = jnp.broadcast_to(padded, (BF16_ROWS, LANES)).astype(BF16)
    start = _dot(padded_rows, (er < ec).astype(BF16))[0:1, :]
    sorted_row = start + rank
    row1 = jnp.sum(jnp.where(pick1, sorted_row, 0.0), axis=1, keepdims=True)
    row2 = jnp.sum(jnp.where(pick2, sorted_row, 0.0), axis=1, keepdims=True)
    meta_ref[...] = (jnp.where(lane == 0, row1, 0.0) + jnp.where(lane == 1, row2, 0.0)
                     + jnp.where(lane == 2, g1, 0.0) + jnp.where(lane == 3, g2, 0.0))
    cnt_ref[0] = jnp.broadcast_to(cnt, (SUBLANES, LANES))


def router(h, w_router, tm=MOE_BLOCK):
    m, d = h.shape
    e = w_router.shape[1]
    w = jnp.pad(w_router, ((0, 0), (0, LANES - e)))
    tm = _row_tile(m, tm)
    row = lambda i: (i, 0)
    return pl.pallas_call(
        functools.partial(_router_kernel, n_experts=e),
        grid=(m // tm,),
        in_specs=[pl.BlockSpec((tm, d), row), pl.BlockSpec((d, LANES), lambda i: (0, 0))],
        out_specs=[pl.BlockSpec((tm, LANES), row), pl.BlockSpec((tm, LANES), row),
                   pl.BlockSpec((1, SUBLANES, LANES), lambda i: (i, 0, 0))],
        out_shape=[jax.ShapeDtypeStruct((m, LANES), F32), jax.ShapeDtypeStruct((m, LANES), F32),
                   jax.ShapeDtypeStruct((m // tm, SUBLANES, LANES), F32)],
        compiler_params=_params("parallel"),
        name="router",
    )(h, w)


def _moe_dense_kernel(x_ref, h_ref, dg_ref, wg_ref, wu_ref, wd_ref, g_ref, b_ref, of_ref, ob_ref, acc_ref):
    e = pl.program_id(1)
    f = pl.program_id(2)
    dg = dg_ref[...]
    lane = lax.broadcasted_iota(jnp.int32, dg.shape, 1)
    w_e = jnp.sum(jnp.where(lane == e, dg, 0.0), axis=1, keepdims=True)
    part = w_e * _swiglu_part(x_ref[...], wg_ref, wu_ref, wd_ref)
    first = jnp.logical_and(e == 0, f == 0)

    @pl.when(first)
    def _():
        acc_ref[...] = part

    @pl.when(jnp.logical_not(first))
    def _():
        acc_ref[...] += part

    @pl.when(jnp.logical_and(e == pl.num_programs(1) - 1, f == pl.num_programs(2) - 1))
    def _():
        out = _layer_norm(ALPHA * h_ref[...] + acc_ref[...], g_ref[...], b_ref[...])
        of_ref[...] = out
        ob_ref[...] = out.astype(BF16)


def moe_dense_res_ln(x, h, dg, w_gu, w_down, g, b, tm=512, nf=2, lead=()):
    m, d = x.shape
    n_e, ff, _ = w_down.shape[-3:]
    tf = ff // nf
    tm = _row_tile(m, tm)
    row = lambda i, e, f: (i, 0)
    fixed = lambda i, e, f: (0, 0)
    return pl.pallas_call(
        _moe_dense_kernel,
        grid=(m // tm, n_e, nf),
        in_specs=[pl.BlockSpec((tm, d), row), pl.BlockSpec((tm, d), row), pl.BlockSpec((tm, LANES), row),
                  _wspec(lead, (None, d, tf), lambda i, e, f: (e, 0, f)),
                  _wspec(lead, (None, d, tf), lambda i, e, f: (e, 0, nf + f)),
                  _wspec(lead, (None, tf, d), lambda i, e, f: (e, f, 0)),
                  pl.BlockSpec((1, d), fixed), pl.BlockSpec((1, d), fixed)],
        out_specs=[pl.BlockSpec((tm, d), row), pl.BlockSpec((tm, d), row)],
        out_shape=[jax.ShapeDtypeStruct((m, d), F32), jax.ShapeDtypeStruct((m, d), BF16)],
        scratch_shapes=[pltpu.VMEM((tm, d), F32)],
        compiler_params=_params("parallel", "arbitrary", "arbitrary"),
        name="moe_dense_res_ln",
    )(x, h, dg, w_gu, w_gu, w_down, g, b)


def _moe_plan(cnt, n_chunks_max, n_tiles):
    nb, n_e = cnt.shape
    cpt = MOE_BLOCK // BF16_ROWS
    chunks = (cnt + BF16_ROWS - 1) // BF16_ROWS
    seg_local = jnp.cumsum(chunks, axis=1) - chunks
    n_chunks = jnp.sum(chunks, axis=1)
    tiles_e = (jnp.sum(chunks, axis=0) + cpt - 1) // cpt
    tile_end = jnp.cumsum(tiles_e)
    seg_global = ((tile_end - tiles_e) * cpt)[None, :] + jnp.cumsum(chunks, axis=0) - chunks
    c = jnp.arange(n_chunks_max, dtype=jnp.int32)
    e_of_c = jnp.sum(c[None, :, None] >= (seg_local + chunks)[:, None, :], axis=2)
    e_of_c = jnp.minimum(e_of_c, n_e - 1).astype(jnp.int32)
    dst = (jnp.take_along_axis(seg_global, e_of_c, axis=1) + c[None, :]
           - jnp.take_along_axis(seg_local, e_of_c, axis=1))
    tile_expert = jnp.sum(jnp.arange(n_tiles, dtype=jnp.int32)[:, None] >= tile_end[None, :], axis=1)
    tile_expert = jnp.minimum(tile_expert, n_e - 1).astype(jnp.int32)
    return (dst.reshape(-1).astype(jnp.int32), n_chunks.astype(jnp.int32), tile_expert,
            tile_end[-1:].astype(jnp.int32))


def _chunk_copy(src_ref, src_chunk, dst_ref, dst_chunk, sem):
    src = src_ref.at[pl.ds(pl.multiple_of(src_chunk * BF16_ROWS, BF16_ROWS), BF16_ROWS), :]
    dst = dst_ref.at[pl.ds(pl.multiple_of(dst_chunk * BF16_ROWS, BF16_ROWS), BF16_ROWS), :]
    return pltpu.make_async_copy(src, dst, sem)


def _dispatch_kernel(dst_ref, nch_ref, x_ref, rows_ref, buf_in_ref, xs_ref, sorted_ref, sem, *, n_chunks_max):
    del buf_in_ref
    b = pl.program_id(0)
    rows = rows_ref[0]
    n_sorted = sorted_ref.shape[0]
    r = lax.broadcasted_iota(jnp.int32, (n_sorted, rows.shape[1]), 0).astype(F32)
    sel = jnp.logical_or(r == rows[0:1, :], r == rows[1:2, :]).astype(BF16)
    sorted_ref[...] = _dot(sel, x_ref[...]).astype(BF16)
    n = nch_ref[b]

    def copy(c):
        return _chunk_copy(sorted_ref, c, xs_ref, dst_ref[b * n_chunks_max + c], sem)

    def start(c, carry):
        copy(c).start()
        return carry

    def wait(c, carry):
        copy(c).wait()
        return carry

    lax.fori_loop(0, n, start, 0)
    lax.fori_loop(0, n, wait, 0)


def _moe_ffn_kernel(te_ref, nt_ref, x_ref, wg_ref, wu_ref, wd_ref, o_ref, acc_ref):
    del te_ref
    i = pl.program_id(0)
    f = pl.program_id(1)

    @pl.when(i < nt_ref[0])
    def _():
        part = _swiglu_part(x_ref[...], wg_ref, wu_ref, wd_ref)

        @pl.when(f == 0)
        def _():
            acc_ref[...] = part

        @pl.when(f > 0)
        def _():
            acc_ref[...] += part

        @pl.when(f == pl.num_programs(1) - 1)
        def _():
            o_ref[...] = acc_ref[...].astype(o_ref.dtype)

    @pl.when(i >= nt_ref[0])
    def _():
        o_ref[...] = jnp.zeros(o_ref.shape, o_ref.dtype)


def _combine_kernel(dst_ref, nch_ref, ys_ref, meta_ref, h_ref, g_ref, b_ref, of_ref, ob_ref, sorted_ref, sem,
                    *, n_chunks_max):
    blk = pl.program_id(0)
    n = nch_ref[blk]

    def copy(c):
        return _chunk_copy(ys_ref, dst_ref[blk * n_chunks_max + c], sorted_ref, c, sem)

    def start(c, carry):
        copy(c).start()
        return carry

    def clear(c, carry):
        sorted_ref[pl.ds(pl.multiple_of(c * BF16_ROWS, BF16_ROWS), BF16_ROWS), :] = jnp.zeros(
            (BF16_ROWS, sorted_ref.shape[1]), sorted_ref.dtype)
        return carry

    def wait(c, carry):
        copy(c).wait()
        return carry

    lax.fori_loop(0, n, start, 0)
    lax.fori_loop(n, n_chunks_max, clear, 0)
    lax.fori_loop(0, n, wait, 0)
    meta = meta_ref[...]
    col = lax.broadcasted_iota(jnp.int32, (meta.shape[0], sorted_ref.shape[0]), 1).astype(F32)
    ys = sorted_ref[...]
    y = (meta[:, 2:3] * _dot((col == meta[:, 0:1]).astype(BF16), ys)
         + meta[:, 3:4] * _dot((col == meta[:, 1:2]).astype(BF16), ys))
    out = _layer_norm(ALPHA * h_ref[...] + y, g_ref[...], b_ref[...])
    of_ref[...] = out
    ob_ref[...] = out.astype(BF16)


def moe_sorted_res_ln(x, h, meta, cnt, w_gu, w_down, g, b, nf=2, lead=()):
    m, d = x.shape
    n_e, ff, _ = w_down.shape[-3:]
    tf = ff // nf
    nb = m // MOE_BLOCK
    n_chunks_max = 2 * MOE_BLOCK // BF16_ROWS + n_e
    n_sorted = n_chunks_max * BF16_ROWS
    cpt = MOE_BLOCK // BF16_ROWS
    n_tiles = -(-nb * n_chunks_max // cpt) + n_e
    dst, n_chunks, tile_expert, n_used = _moe_plan(cnt[:, 0, :n_e].astype(jnp.int32), n_chunks_max, n_tiles)
    rows_t = jnp.swapaxes(meta[:, :SUBLANES].reshape(nb, MOE_BLOCK, SUBLANES), 1, 2)

    any_space = pl.BlockSpec(memory_space=pl.ANY)
    xs = pl.pallas_call(
        functools.partial(_dispatch_kernel, n_chunks_max=n_chunks_max),
        grid_spec=pltpu.PrefetchScalarGridSpec(
            num_scalar_prefetch=2,
            grid=(nb,),
            in_specs=[pl.BlockSpec((MOE_BLOCK, d), lambda i, *_: (i, 0)),
                      pl.BlockSpec((1, SUBLANES, MOE_BLOCK), lambda i, *_: (i, 0, 0)),
                      any_space],
            out_specs=any_space,
            scratch_shapes=[pltpu.VMEM((n_sorted, d), BF16), pltpu.SemaphoreType.DMA(())],
        ),
        out_shape=jax.ShapeDtypeStruct((n_tiles * MOE_BLOCK, d), BF16),
        input_output_aliases={4: 0},
        compiler_params=_params("arbitrary"),
        name="moe_dispatch",
    )(dst, n_chunks, x, rows_t, jnp.zeros((n_tiles * MOE_BLOCK, d), BF16))

    def tile(i, f, te, nt):
        return jnp.maximum(jnp.minimum(i, nt[0] - 1), 0)

    ys = pl.pallas_call(
        _moe_ffn_kernel,
        grid_spec=pltpu.PrefetchScalarGridSpec(
            num_scalar_prefetch=2,
            grid=(n_tiles, nf),
            in_specs=[pl.BlockSpec((MOE_BLOCK, d), lambda i, f, te, nt: (tile(i, f, te, nt), 0)),
                      _wspec(lead, (None, d, tf), lambda i, f, te, nt: (te[tile(i, f, te, nt)], 0, f)),
                      _wspec(lead, (None, d, tf), lambda i, f, te, nt: (te[tile(i, f, te, nt)], 0, nf + f)),
                      _wspec(lead, (None, tf, d), lambda i, f, te, nt: (te[tile(i, f, te, nt)], f, 0))],
            out_specs=pl.BlockSpec((MOE_BLOCK, d), lambda i, f, te, nt: (i, 0)),
            scratch_shapes=[pltpu.VMEM((MOE_BLOCK, d), F32)],
        ),
        out_shape=jax.ShapeDtypeStruct((n_tiles * MOE_BLOCK, d), BF16),
        compiler_params=_params("arbitrary", "arbitrary"),
        name="moe_expert_ffn",
    )(tile_expert, n_used, xs, w_gu, w_gu, w_down)

    row = lambda i, *_: (i, 0)
    fixed = lambda i, *_: (0, 0)
    return pl.pallas_call(
        functools.partial(_combine_kernel, n_chunks_max=n_chunks_max),
        grid_spec=pltpu.PrefetchScalarGridSpec(
            num_scalar_prefetch=2,
            grid=(nb,),
            in_specs=[any_space, pl.BlockSpec((MOE_BLOCK, LANES), row), pl.BlockSpec((MOE_BLOCK, d), row),
                      pl.BlockSpec((1, d), fixed), pl.BlockSpec((1, d), fixed)],
            out_specs=[pl.BlockSpec((MOE_BLOCK, d), row), pl.BlockSpec((MOE_BLOCK, d), row)],
            scratch_shapes=[pltpu.VMEM((n_sorted, d), BF16), pltpu.SemaphoreType.DMA(())],
        ),
        out_shape=[jax.ShapeDtypeStruct((m, d), F32), jax.ShapeDtypeStruct((m, d), BF16)],
        compiler_params=_params("arbitrary"),
        name="moe_combine_res_ln",
    )(dst, n_chunks, ys, meta, h, g, b)


def _log_sigmoid(x):
    return jnp.minimum(x, 0.0) - jnp.log(1.0 + jnp.exp(-jnp.abs(x)))


def _mlstm_kernel(q_ref, k_ref, v_ref, o_ref, gc_ref, gr_ref, bc_ref, br_ref, gh_ref, c0_ref, n0_ref, m0_ref,
                  hg_ref, c_ref, n_ref, m_ref, *, chunk, t_valid, nh, dh):
    ci = pl.program_id(1)

    @pl.when(ci == 0)
    def _():
        c_ref[...] = c0_ref[...]
        n_ref[...] = n0_ref[...]
        m_ref[...] = m0_ref[...]

    gc = gc_ref[0] + bc_ref[...]
    gr = gr_ref[0] + br_ref[...]
    li_c, lf_c = gc[:, :nh], _log_sigmoid(gc[:, nh:])
    li_r, lf_r = gr[:nh, :], _log_sigmoid(gr[nh:, :])
    if t_valid < chunk:
        tok_c = lax.broadcasted_iota(jnp.int32, li_c.shape, 0) < t_valid
        tok_r = lax.broadcasted_iota(jnp.int32, li_r.shape, 1) < t_valid
        li_c, lf_c = jnp.where(tok_c, li_c, NEG), jnp.where(tok_c, lf_c, 0.0)
        li_r, lf_r = jnp.where(tok_r, li_r, NEG), jnp.where(tok_r, lf_r, 0.0)
    row = lax.broadcasted_iota(jnp.int32, (chunk, chunk), 0)
    col = lax.broadcasted_iota(jnp.int32, (chunk, chunk), 1)
    causal = col <= row
    hi = lax.Precision.HIGHEST
    b_c = jnp.dot(causal.astype(F32), lf_c, preferred_element_type=F32, precision=hi)
    b_r = jnp.dot(lf_r, (row <= col).astype(F32), preferred_element_type=F32, precision=hi)

    mm = q_ref.dtype
    staged = []
    for h in range(nh):
        hs = slice(h * dh, (h + 1) * dh)
        bcol, icol = b_c[:, h:h + 1], li_c[:, h:h + 1]
        brow, irow = b_r[h:h + 1, :], li_r[h:h + 1, :]
        m_prev = m_ref[0, h, :, 0:1]
        c_prev = c_ref[0, h]
        q = q_ref[:, hs]
        kf = k_ref[:, hs].astype(F32) * (dh ** -0.5)
        dmat = jnp.where(causal, bcol - brow + irow, NEG)
        inter = bcol + m_prev
        mt = jnp.maximum(inter, jnp.max(dmat, axis=1, keepdims=True))
        w_inter = jnp.exp(inter - mt)
        qk = _dot_nt(q, kf.astype(mm))
        qc = _dot(q, c_prev.astype(mm))
        qn = jnp.sum(q.astype(F32) * n_ref[0, h], axis=1, keepdims=True)
        staged.append((hs, bcol, icol, m_prev, c_prev, q, kf, dmat, mt, w_inter, qk, qc, qn))

    for h, (hs, bcol, icol, m_prev, c_prev, q, kf, dmat, mt, w_inter, qk, qc, qn) in enumerate(staged):
        n_prev = n_ref[0, h]
        v = v_ref[:, hs]
        b_last = bcol[chunk - 1:chunk, :]
        gdec = b_last - bcol + icol
        m_new = jnp.maximum(b_last + m_prev, jnp.max(gdec, axis=0, keepdims=True))
        decay = jnp.exp(b_last + m_prev - m_new)
        kw = kf * jnp.exp(gdec - m_new)
        c_ref[0, h] = decay * c_prev + _dot_tn(kw.astype(mm), v)
        n_ref[0, h] = decay * n_prev + jnp.sum(kw, axis=0, keepdims=True)
        m_ref[0, h] = jnp.broadcast_to(m_new, (1, LANES))

    readout = []
    for h, (hs, bcol, icol, m_prev, c_prev, q, kf, dmat, mt, w_inter, qk, qc, qn) in enumerate(staged):
        s = qk * jnp.exp(dmat - mt)
        num = w_inter * qc + _dot(s.astype(mm), v_ref[:, hs])
        den = w_inter * qn + jnp.sum(s, axis=1, keepdims=True)
        readout.append((num, den))

    for h, (num, den) in enumerate(readout):
        hs, mt = staged[h][0], staged[h][8]
        hh = num * (1.0 / jnp.maximum(jnp.abs(den), jnp.exp(-mt)))
        mu = jnp.mean(hh, axis=1, keepdims=True)
        hc = hh - mu
        var = jnp.mean(hc * hc, axis=1, keepdims=True)
        hn = hc * lax.rsqrt(var + LN_EPS) * gh_ref[h:h + 1, :]
        hg_ref[:, hs] = (hn * _sigmoid(o_ref[:, hs])).astype(hg_ref.dtype)


def mlstm(qkv, o, gates, b_gate, g_head, c0, n0, m0, t_valid, chunk):
    bsz, t_len, d3 = qkv.shape
    d = d3 // 3
    nh = g_head.shape[0]
    dh = d // nh
    nc = t_len // chunk
    gates_t = jnp.swapaxes(gates, 1, 2)
    seq = lambda b, c: (b, 0, 0, 0)
    fixed = lambda b, c: (0, 0)
    kern = functools.partial(_mlstm_kernel, chunk=chunk, t_valid=t_valid, nh=nh, dh=dh)
    hg, c, n, m = pl.pallas_call(
        kern,
        grid=(bsz, nc),
        in_specs=[pl.BlockSpec((None, chunk, d), lambda b, c: (b, c, 0)),
                  pl.BlockSpec((None, chunk, d), lambda b, c: (b, c, 1)),
                  pl.BlockSpec((None, chunk, d), lambda b, c: (b, c, 2)),
                  pl.BlockSpec((None, chunk, d), lambda b, c: (b, c, 0)),
                  pl.BlockSpec((1, chunk, 2 * nh), lambda b, c: (b, c, 0)),
                  pl.BlockSpec((1, 2 * nh, chunk), lambda b, c: (b, 0, c)),
                  pl.BlockSpec((1, 2 * nh), fixed), pl.BlockSpec((2 * nh, 1), fixed),
                  pl.BlockSpec((nh, dh), fixed),
                  pl.BlockSpec((1, nh, dh, dh), seq), pl.BlockSpec((1, nh, 1, dh), seq),
                  pl.BlockSpec((1, nh, 1, LANES), seq)],
        out_specs=[pl.BlockSpec((None, chunk, d), lambda b, c: (b, c, 0)),
                   pl.BlockSpec((1, nh, dh, dh), seq), pl.BlockSpec((1, nh, 1, dh), seq),
                   pl.BlockSpec((1, nh, 1, LANES), seq)],
        out_shape=[jax.ShapeDtypeStruct((bsz, t_len, d), qkv.dtype),
                   jax.ShapeDtypeStruct((bsz, nh, dh, dh), F32),
                   jax.ShapeDtypeStruct((bsz, nh, 1, dh), F32),
                   jax.ShapeDtypeStruct((bsz, nh, 1, LANES), F32)],
        compiler_params=_params("parallel", "arbitrary"),
        name="mlstm",
    )(qkv, qkv, qkv, o, gates, gates_t, b_gate.reshape(1, 2 * nh), b_gate.reshape(2 * nh, 1), g_head,
      c0, n0.reshape(bsz, nh, 1, dh), jnp.broadcast_to(m0[:, :, None, None], (bsz, nh, 1, LANES)))
    return hg, c, n.reshape(bsz, nh, dh), m[:, :, 0, 0]


def _lambda_value(lam_ref, lam_init):
    lp = lam_ref[...]
    a = jnp.sum(lp[0:1, :] * lp[1:2, :], axis=1, keepdims=True)
    b = jnp.sum(lp[2:3, :] * lp[3:4, :], axis=1, keepdims=True)
    return jnp.exp(a) - jnp.exp(b) + lam_init


def _split_queries(q, dh):
    lane = lax.broadcasted_iota(jnp.int32, q.shape, 1)
    zero = jnp.zeros_like(q)
    return jnp.concatenate([jnp.where(lane < dh, q, zero), jnp.where(lane >= dh, q, zero)], axis=0)


def _flash_step(qi, q_ref, k_ref, vt_ref, lam_ref, g_ref, o_ref, qs_ref, s_ref, m_ref, l_ref, acc_ref,
                *, blk, dh, lam_init, interleave=()):
    qs_ref[...] = _split_queries(q_ref[...], dh)
    m_ref[...] = jnp.full(m_ref.shape, NEG, F32)
    l_ref[...] = jnp.zeros(l_ref.shape, F32)
    acc_ref[...] = jnp.zeros(acc_ref.shape, F32)
    sub = 2 * blk // ATTN_QSPLIT

    def scores(ki, c):
        k = k_ref[pl.ds(pl.multiple_of(ki * blk, blk), blk), :]
        return _dot_nt(k, qs_ref[c * sub:(c + 1) * sub, :])

    def update(ki, c, s, diagonal):
        cs = slice(c * sub, (c + 1) * sub)
        vt = vt_ref[:, pl.ds(pl.multiple_of(ki * blk, blk), blk)]
        if diagonal:
            key = lax.broadcasted_iota(jnp.int32, s.shape, 0)
            qry = (lax.broadcasted_iota(jnp.int32, s.shape, 1) + c * sub) % blk
            s = jnp.where(key <= qry, s, NEG)
        m_prev = m_ref[:, cs]
        m_new = jnp.maximum(m_prev, jnp.max(s, axis=0, keepdims=True))
        alpha = jnp.exp2(m_prev - m_new)
        p = jnp.exp2(s - m_new)
        l_ref[:, cs] = alpha * l_ref[:, cs] + jnp.sum(p, axis=0, keepdims=True)
        acc_ref[:, cs] = alpha * acc_ref[:, cs] + _dot(vt, p.astype(BF16))
        m_ref[:, cs] = m_new

    def block(ki, diagonal):
        pending = [s_ref[i] for i in range(ATTN_LOOKAHEAD)]
        for c in range(ATTN_QSPLIT):
            nxt = c + ATTN_LOOKAHEAD
            if nxt < ATTN_QSPLIT:
                pending.append(scores(ki, nxt))
            elif not diagonal:
                pending.append(scores(ki + 1, nxt - ATTN_QSPLIT))
            if diagonal and c < len(interleave):
                interleave[c]()
            update(ki, c, pending.pop(0), diagonal)
        if diagonal:
            for extra in interleave[ATTN_QSPLIT:]:
                extra()
        for i, s in enumerate(pending):
            s_ref[i] = s

    def body(ki, carry):
        block(ki, False)
        return carry

    for i in range(ATTN_LOOKAHEAD):
        s_ref[i] = scores(0, i)
    lax.fori_loop(0, qi, body, 0)
    block(qi, True)
    lam = _lambda_value(lam_ref, lam_init)
    acc = acc_ref[...]
    r = 1.0 / l_ref[...]
    o = acc[:, :blk] * r[:, :blk] - lam * (acc[:, blk:] * r[:, blk:])
    ms = jnp.mean(o * o, axis=0, keepdims=True)
    o = o * lax.rsqrt(ms + LN_EPS) * (g_ref[...] * (1.0 - lam_init))
    o_ref[...] = o.T.astype(o_ref.dtype)


def _decode_stages(j, n_steps, q_ref, k_refs, v_refs, kn_ref, vn_ref, lam_ref, g_ref, o_ref, qs_ref, m_ref, l_ref,
                   acc_ref, s_ref, p_ref, alpha_ref, *, nh, dh, lam_init):
    page = k_refs[0].shape[0]
    rows = page * nh

    def begin():
        @pl.when(j == 0)
        def _():
            qs = _split_queries(q_ref[0], dh) * (dh ** -0.5)
            qs_ref[...] = qs.astype(BF16)
            kn = jnp.concatenate([kn_ref[0, 0], kn_ref[0, 0]], axis=0)
            m_ref[...] = jnp.sum(qs * kn, axis=1, keepdims=True)
            l_ref[...] = jnp.ones(l_ref.shape, F32)
            acc_ref[...] = jnp.concatenate([vn_ref[0, 0], vn_ref[0, 0]], axis=0)

    def scores_of(first, count):
        def run():
            if count == 0:
                return
            qs = qs_ref[...]
            s = jnp.concatenate([_dot_nt(qs, k_ref[...].reshape(rows, 2 * dh).astype(BF16))
                                 for k_ref in k_refs[first:first + count]], axis=1)
            row = lax.broadcasted_iota(jnp.int32, s.shape, 0) % nh
            col = lax.broadcasted_iota(jnp.int32, s.shape, 1) % nh
            s_ref[:, first * rows:(first + count) * rows] = jnp.where(row == col, s, NEG)
        return run

    half = (len(k_refs) + 1) // 2
    scores_lo, scores_hi = scores_of(0, half), scores_of(half, len(k_refs) - half)

    def softmax():
        s = s_ref[...]
        m_prev = m_ref[...]
        m_new = jnp.maximum(m_prev, jnp.max(s, axis=1, keepdims=True))
        alpha = jnp.exp(m_prev - m_new)
        p = jnp.exp(s - m_new).astype(BF16)
        l_ref[...] = alpha * l_ref[...] + jnp.sum(p.astype(F32), axis=1, keepdims=True)
        m_ref[...] = m_new
        alpha_ref[...] = alpha
        p_ref[...] = p

    def accumulate():
        pv = _dot(p_ref[:, :rows], v_refs[0][...].reshape(rows, 2 * dh).astype(BF16))
        for i in range(1, len(v_refs)):
            pv += _dot(p_ref[:, i * rows:(i + 1) * rows], v_refs[i][...].reshape(rows, 2 * dh).astype(BF16))
        acc_ref[...] = alpha_ref[...] * acc_ref[...] + pv

    def finish():
        @pl.when(j == n_steps - 1)
        def _():
            lam = _lambda_value(lam_ref, lam_init)
            acc = acc_ref[...]
            r = 1.0 / l_ref[...]
            o = acc[:nh] * r[:nh] - lam * (acc[nh:] * r[nh:])
            ms = jnp.mean(o * o, axis=1, keepdims=True)
            o_ref[0] = o * lax.rsqrt(ms + LN_EPS) * g_ref[...] * (1.0 - lam_init)

    return begin, scores_lo, scores_hi, softmax, accumulate, finish


def _attention_kernel(pt_ref, q_ref, k_ref, vt_ref, lam_ref, gcol_ref, qd_ref, ck_ref, cv_ref, kn_ref, vn_ref,
                      grow_ref, o_ref, od_ref, qs_ref, s_ref, m_ref, l_ref, acc_ref, dqs_ref, dm_ref, dl_ref,
                      dacc_ref, ds_ref, dp_ref, dalpha_ref, kbuf_ref, vbuf_ref, sem,
                      *, blk, dh, nh, lam_init, pps, steps_per_seq, n_decode_steps, n_grid_steps):
    step = (pl.program_id(0) * pl.num_programs(1) + pl.program_id(1)) * pl.num_programs(2) + pl.program_id(2)
    slot = step % 2

    def page_copies(t, into):
        seq, first = t // steps_per_seq, (t % steps_per_seq) * pps
        for p in range(pps):
            phys = pt_ref[seq, first + p]
            yield pltpu.make_async_copy(ck_ref.at[phys], kbuf_ref.at[into, p], sem.at[into])
            yield pltpu.make_async_copy(cv_ref.at[phys], vbuf_ref.at[into, p], sem.at[into])

    @pl.when(step == 0)
    def _():
        for copy in page_copies(step, slot):
            copy.start()

    @pl.when(step + 1 < n_decode_steps)
    def _():
        for copy in page_copies(step + 1, 1 - slot):
            copy.start()

    @pl.when(step < n_decode_steps)
    def _():
        for copy in page_copies(step, slot):
            copy.wait()

    kd_refs = [kbuf_ref.at[slot, p] for p in range(pps)]
    vd_refs = [vbuf_ref.at[slot, p] for p in range(pps)]
    stages = _decode_stages(step % steps_per_seq, steps_per_seq, qd_ref, kd_refs, vd_refs, kn_ref, vn_ref, lam_ref,
                            grow_ref, od_ref, dqs_ref, dm_ref, dl_ref, dacc_ref, ds_ref, dp_ref, dalpha_ref,
                            nh=nh, dh=dh, lam_init=lam_init)
    if n_decode_steps != n_grid_steps:
        stages = [functools.partial(pl.when(step < n_decode_steps), f) for f in stages]
    begin, scores_lo, scores_hi, softmax, accumulate, finish = stages

    def scores_hi_softmax():
        scores_hi()
        softmax()

    begin()
    _flash_step(pl.program_id(2), q_ref, k_ref, vt_ref, lam_ref, gcol_ref, o_ref, qs_ref, s_ref, m_ref, l_ref,
                acc_ref, blk=blk, dh=dh, lam_init=lam_init, interleave=(scores_lo, scores_hi_softmax, accumulate))
    finish()


def diff_attention(q, k, vt, q_dec, k_new, v_new, cache_k, cache_v, page_table, lam_p, g_subln, lam_init,
                   blk=ATTN_BLOCK):
    bsz, t_len, d = q.shape
    n_seq, nh, hw = q_dec.shape
    dh = hw // 2
    blk = min(blk, t_len)
    nq = t_len // blk
    n_steps = bsz * nh * nq
    n_pages = page_table.shape[1]
    page = cache_k.shape[1]
    pps = min(p for p in range(1, n_pages + 1) if n_pages % p == 0 and n_seq * (n_pages // p) <= n_steps)
    steps_per_seq = n_pages // pps
    n_decode_steps = n_seq * steps_per_seq

    def decode_pos(b, h, i):
        step = jnp.minimum((b * nh + h) * nq + i, n_decode_steps - 1)
        return step // steps_per_seq, step % steps_per_seq

    def seq3(b, h, i, pt):
        return (decode_pos(b, h, i)[0], 0, 0)

    def seq4(b, h, i, pt):
        return (decode_pos(b, h, i)[0], 0, 0, 0)

    in_hbm = pl.BlockSpec(memory_space=pl.ANY)
    fixed = lambda b, h, i, pt: (0, 0)
    grid_spec = pltpu.PrefetchScalarGridSpec(
        num_scalar_prefetch=1,
        grid=(bsz, nh, nq),
        in_specs=([pl.BlockSpec((None, blk, hw), lambda b, h, i, pt: (b, i, h)),
                   pl.BlockSpec((None, t_len, hw), lambda b, h, i, pt: (b, 0, h)),
                   pl.BlockSpec((None, hw, t_len), lambda b, h, i, pt: (b, h, 0)),
                   pl.BlockSpec(lam_p.shape, fixed), pl.BlockSpec((hw, 1), fixed),
                   pl.BlockSpec((1, nh, hw), seq3), in_hbm, in_hbm,
                   pl.BlockSpec((1, 1, nh, hw), seq4), pl.BlockSpec((1, 1, nh, hw), seq4),
                   pl.BlockSpec((1, hw), fixed)]),
        out_specs=[pl.BlockSpec((None, blk, hw), lambda b, h, i, pt: (b, i, h)),
                   pl.BlockSpec((1, nh, hw), seq3)],
        scratch_shapes=[pltpu.VMEM((2 * blk, hw), BF16),
                        pltpu.VMEM((ATTN_LOOKAHEAD, blk, 2 * blk // ATTN_QSPLIT), F32),
                        pltpu.VMEM((1, 2 * blk), F32), pltpu.VMEM((1, 2 * blk), F32), pltpu.VMEM((hw, 2 * blk), F32),
                        pltpu.VMEM((2 * nh, hw), BF16), pltpu.VMEM((2 * nh, 1), F32),
                        pltpu.VMEM((2 * nh, 1), F32), pltpu.VMEM((2 * nh, hw), F32),
                        pltpu.VMEM((2 * nh, pps * page * nh), F32), pltpu.VMEM((2 * nh, pps * page * nh), BF16),
                        pltpu.VMEM((2 * nh, 1), F32),
                        pltpu.VMEM((2, pps, page, nh, hw), F32), pltpu.VMEM((2, pps, page, nh, hw), F32),
                        pltpu.SemaphoreType.DMA((2,))],
    )
    return pl.pallas_call(
        functools.partial(_attention_kernel, blk=blk, dh=dh, nh=nh, lam_init=lam_init, pps=pps,
                          steps_per_seq=steps_per_seq, n_decode_steps=n_decode_steps, n_grid_steps=n_steps),
        grid_spec=grid_spec,
        out_shape=[jax.ShapeDtypeStruct((bsz, t_len, d), BF16), jax.ShapeDtypeStruct((n_seq, nh, hw), F32)],
        compiler_params=_params("arbitrary", "arbitrary", "arbitrary"),
        name="diff_attention",
    )(page_table, q, k, vt, lam_p, g_subln.reshape(hw, 1), q_dec, cache_k, cache_v, k_new, v_new,
      g_subln.reshape(1, hw))


def _trunk(x, state, decode, wts):
    bsz, t_len, d = x.shape
    m_rows = bsz * t_len
    n_a = wts["w_in"].shape[0]
    hw = wts["g_subln"].shape[-1]
    n_heads = d // hw
    act = wts["w_in"].dtype
    pick = (lambda h32, hbf: hbf) if act == BF16 else (lambda h32, hbf: h32)
    h32 = x.reshape(m_rows, d)
    ha = h32.astype(act)
    cs, ns, ms = [], [], []
    k32 = v32 = kbf = vt = None
    for i in range(DEPTH):
        ln_g, ln_b = wts["ln_g"][i], wts["ln_b"][i]
        if i < n_a:
            (qkv,) = matmul(ha, wts["w_in"], [act], lead=(i,), n=3 * d)
            (og,) = matmul(ha, wts["w_in"], [F32], lead=(i,), col0=3, n=d)
            (gates,) = matmul(ha, wts["w_gates"], [F32], lead=(i,))
            chunk = MLSTM_CHUNK if t_len % MLSTM_CHUNK == 0 else BF16_ROWS
            t_pad = -(-t_len // chunk) * chunk
            gates = gates[:, :2 * A_HEADS].reshape(bsz, t_len, 2 * A_HEADS)
            og = og.reshape(bsz, t_len, d)
            qkv = qkv.reshape(bsz, t_len, 3 * d)
            if t_pad != t_len:
                pad = ((0, 0), (0, t_pad - t_len), (0, 0))
                qkv, og, gates = jnp.pad(qkv, pad), jnp.pad(og, pad), jnp.pad(gates, pad)
            if state is None:
                dh = d // A_HEADS
                c0 = jnp.zeros((bsz, A_HEADS, dh, dh), F32)
                n0 = jnp.zeros((bsz, A_HEADS, dh), F32)
                m0 = jnp.zeros((bsz, A_HEADS), F32)
            else:
                c0, n0, m0 = state[0][i], state[1][i], state[2][i]
            hg, c, n, m = mlstm(qkv, og, gates, wts["b_gate"][i], wts["g_head"][i], c0, n0, m0, t_len, chunk)
            cs.append(c)
            ns.append(n)
            ms.append(m)
            mix_in = hg[:, :t_len].reshape(m_rows, d)
            w_mix, mix_lead = wts["w_out"], (i,)
        else:
            j = i - n_a
            if j == 0:
                if not decode:
                    k32, kbf = matmul(ha, wts["w_kv"], [F32, BF16], n=d, heads=n_heads)
                    (v32,) = matmul(ha, wts["w_kv"], [F32], col0=1, n=d, heads=n_heads)
                    vt = matmul_t(ha.reshape(bsz, t_len, d), wts["w_v_t"])
                else:
                    (k32,) = matmul(ha, wts["w_kv"], [F32], n=d, heads=n_heads)
                    (v32,) = matmul(ha, wts["w_kv"], [F32], col0=1, n=d, heads=n_heads)
            if not decode:
                (q,) = matmul(ha, wts["w_q"], [BF16], lead=(j,), scale=math.log2(math.e) * (hw // 2) ** -0.5)
                o = yield {"q": q.reshape(bsz, t_len, d), "k": kbf.reshape(bsz, t_len, d), "vt": vt}
                mix_in = o.reshape(m_rows, d)
            else:
                (q,) = matmul(ha, wts["w_q"], [F32], lead=(j,))
                o = yield {"q": q.reshape(bsz, n_heads, hw), "k_new": k32.reshape(bsz, 1, n_heads, hw),
                           "v_new": v32.reshape(bsz, 1, n_heads, hw)}
                mix_in = o.reshape(m_rows, d).astype(act)
            w_mix, mix_lead = wts["w_o"], (j,)
        h32, hbf = matmul_res_ln(mix_in, w_mix, h32, ln_g[0:1], ln_b[0:1], lead=mix_lead)
        ha = pick(h32, hbf)
        if i % 2 == 0:
            h32, hbf = ffn_res_ln(ha, h32, wts["w_ffn_gu"], wts["w_ffn_down"], ln_g[1:2], ln_b[1:2], lead=(i // 2,))
        else:
            dg, meta, cnt = router(h32, wts["w_router"][i // 2])
            w_gu, w_down, lead = wts["w_moe_gu"], wts["w_moe_down"], (i // 2,)
            if m_rows >= MOE_MIN_ROWS and m_rows % MOE_BLOCK == 0:
                h32, hbf = moe_sorted_res_ln(hbf, h32, meta, cnt, w_gu, w_down, ln_g[1:2], ln_b[1:2], lead=lead)
            else:
                h32, hbf = moe_dense_res_ln(hbf, h32, dg, w_gu, w_down, ln_g[1:2], ln_b[1:2], lead=lead)
        ha = pick(h32, hbf)
    y = h32.reshape(bsz, t_len, d)
    k_rows = k32.reshape(bsz, t_len, n_heads, hw)
    v_rows = v32.reshape(bsz, t_len, n_heads, hw)
    return y, jnp.stack(cs), jnp.stack(ns), jnp.stack(ms), k_rows, v_rows


def _resume(gen, value):
    try:
        return gen.send(value), None
    except StopIteration as done:
        return None, done.value


def kernel(x_prompt, x_sample, state_c, state_n, state_m, cache_k, cache_v, page_table, ln_g, ln_b, w_in_a, b_gate_a, g_head_a, w_out_a, w_kv, w_q_b, lam_b, g_subln_b, w_o_b, w_ffn_gu, w_ffn_down, w_router, w_moe_gu, w_moe_down):
    d = x_prompt.shape[-1]
    n_gate = w_in_a.shape[-1] - 4 * d
    w_gates = jnp.pad(w_in_a[:, :, 4 * d:], ((0, 0), (0, 0), (0, LANES - n_gate)))

    moe_bf16 = {"w_moe_gu": w_moe_gu.astype(BF16), "w_moe_down": w_moe_down.astype(BF16)}

    def weights(cast):
        return {
            "ln_g": ln_g, "ln_b": ln_b, "b_gate": b_gate_a, "g_head": g_head_a, "w_router": w_router,
            "g_subln": g_subln_b,
            "w_in": cast(w_in_a), "w_gates": cast(w_gates), "w_out": cast(w_out_a), "w_kv": cast(w_kv),
            "w_v_t": cast(w_kv[:, d:].T), "w_q": cast(w_q_b), "w_o": cast(w_o_b), "w_ffn_gu": cast(w_ffn_gu),
            "w_ffn_down": cast(w_ffn_down), **moe_bf16,
        }

    prompt = _trunk(x_prompt, None, False, weights(lambda a: a.astype(BF16)))
    sample = _trunk(x_sample, (state_c, state_n, state_m), True, weights(lambda a: a))
    n_a = w_in_a.shape[0]
    req_p, req_s = next(prompt), next(sample)
    out_p = out_s = None
    for j in range(w_q_b.shape[0]):
        o_p, o_s = diff_attention(req_p["q"], req_p["k"], req_p["vt"], req_s["q"], req_s["k_new"], req_s["v_new"],
                                  cache_k, cache_v, page_table, lam_b[j], g_subln_b[j], _lambda_init(n_a + j))
        req_p, out_p = _resume(prompt, o_p)
        req_s, out_s = _resume(sample, o_s)
    y_p, c_p, n_p, m_p, k_p, v_p = out_p
    y_s, c_s, n_s, m_s, k_s, v_s = out_s
    return (y_p, y_s, c_p, n_p, m_p, k_p, v_p, c_s, n_s, m_s, k_s, v_s)
```

```python
import functools
import math

import jax
import jax.numpy as jnp
from jax import lax
from jax.experimental import pallas as pl
from jax.experimental.pallas import tpu as pltpu

F32 = jnp.float32
BF16 = jnp.bfloat16

DEPTH = 4
ALPHA = (2 * DEPTH) ** 0.25
LN_EPS = 1e-5
NEG = -1e30
LANES = 128
SUBLANES = 8
BF16_ROWS = 16
VMEM_LIMIT = 56 * 1024 * 1024

A_HEADS = 4
MLSTM_CHUNK = 256
ATTN_BLOCK = 512
ATTN_QSPLIT = 2
ATTN_LOOKAHEAD = 2
MOE_BLOCK = 512
MOE_MIN_ROWS = 1024


def _lambda_init(layer_idx):
    return 0.8 - 0.6 * math.exp(-0.3 * layer_idx)


def _params(*sem):
    return pltpu.CompilerParams(dimension_semantics=sem, vmem_limit_bytes=VMEM_LIMIT)


def _contract(a, b, dims):
    precision = lax.Precision.HIGHEST if a.dtype == F32 and b.dtype == F32 else None
    return lax.dot_general(a, b, (dims, ((), ())), preferred_element_type=F32, precision=precision)


def _dot(a, b):
    return _contract(a, b, ((1,), (0,)))


def _dot_nt(a, b):
    return _contract(a, b, ((1,), (1,)))


def _dot_tn(a, b):
    return _contract(a, b, ((0,), (0,)))


def _layer_norm(y, g, b):
    mu = jnp.mean(y, axis=-1, keepdims=True)
    yc = y - mu
    var = jnp.mean(yc * yc, axis=-1, keepdims=True)
    return yc * lax.rsqrt(var + LN_EPS) * g + b


def _sigmoid(x):
    return 0.5 * jnp.tanh(0.5 * x) + 0.5


def _row_tile(m, want):
    return want if m % want == 0 else m


def _wspec(lead, block, index):
    return pl.BlockSpec((None,) * len(lead) + block, lambda *a: tuple(lead) + index(*a))


def _mm_kernel(x_ref, w_ref, *o_refs, scale):
    acc = _dot(x_ref[...], w_ref[...])
    if scale != 1.0:
        acc = acc * scale
    for o_ref in o_refs:
        o_ref[...] = acc.astype(o_ref.dtype).reshape(o_ref.shape)


def matmul(x, w, out_dtypes, tm=1024, tn=1024, scale=1.0, lead=(), col0=0, n=None, heads=None):
    m, k = x.shape
    n = w.shape[-1] if n is None else n
    tm = _row_tile(m, tm)
    tn = _row_tile(n, tn)
    out_specs = [pl.BlockSpec((tm, tn), lambda i, j: (i, j)) for _ in out_dtypes]
    out_shape = [jax.ShapeDtypeStruct((m, n), d) for d in out_dtypes]
    if heads is not None:
        assert tn == n
        out_specs[0] = pl.BlockSpec((tm, heads, n // heads), lambda i, j: (i, 0, 0))
        out_shape[0] = jax.ShapeDtypeStruct((m, heads, n // heads), out_dtypes[0])
    return pl.pallas_call(
        functools.partial(_mm_kernel, scale=scale),
        grid=(m // tm, n // tn),
        in_specs=[pl.BlockSpec((tm, k), lambda i, j: (i, 0)),
                  _wspec(lead, (k, tn), lambda i, j: (0, col0 + j))],
        out_specs=out_specs,
        out_shape=out_shape,
        compiler_params=_params("parallel", "parallel"),
        name="matmul",
    )(x, w)


def _mm_t_kernel(x_ref, wt_ref, o_ref):
    o_ref[...] = _dot_nt(wt_ref[...], x_ref[...]).astype(o_ref.dtype)


def matmul_t(x, wt, tm=1024, tn=1024):
    bsz, t_len, k = x.shape
    n = wt.shape[0]
    tm = _row_tile(t_len, tm)
    tn = _row_tile(n, tn)
    return pl.pallas_call(
        _mm_t_kernel,
        grid=(bsz, t_len // tm, n // tn),
        in_specs=[pl.BlockSpec((None, tm, k), lambda b, i, j: (b, i, 0)),
                  pl.BlockSpec((tn, k), lambda b, i, j: (j, 0))],
        out_specs=pl.BlockSpec((None, tn, tm), lambda b, i, j: (b, j, i)),
        out_shape=jax.ShapeDtypeStruct((bsz, n, t_len), BF16),
        compiler_params=_params("parallel", "parallel", "parallel"),
        name="matmul_t",
    )(x, wt)


def _mm_res_ln_kernel(x_ref, w_ref, h_ref, g_ref, b_ref, of_ref, ob_ref):
    y = ALPHA * h_ref[...] + _dot(x_ref[...], w_ref[...])
    out = _layer_norm(y, g_ref[...], b_ref[...])
    of_ref[...] = out
    ob_ref[...] = out.astype(BF16)


def matmul_res_ln(x, w, h, g, b, tm=1024, lead=()):
    m, k = x.shape
    d = w.shape[-1]
    tm = _row_tile(m, tm)
    row = lambda i: (i, 0)
    fixed = lambda i: (0, 0)
    return pl.pallas_call(
        _mm_res_ln_kernel,
        grid=(m // tm,),
        in_specs=[pl.BlockSpec((tm, k), row), _wspec(lead, (k, d), fixed), pl.BlockSpec((tm, d), row),
                  pl.BlockSpec((1, d), fixed), pl.BlockSpec((1, d), fixed)],
        out_specs=[pl.BlockSpec((tm, d), row), pl.BlockSpec((tm, d), row)],
        out_shape=[jax.ShapeDtypeStruct((m, d), F32), jax.ShapeDtypeStruct((m, d), BF16)],
        compiler_params=_params("parallel"),
        name="matmul_res_ln",
    )(x, w, h, g, b)


def _swiglu_part(x, wg_ref, wu_ref, wd_ref):
    gate = _dot(x, wg_ref[...])
    up = _dot(x, wu_ref[...])
    act = (gate * _sigmoid(gate) * up).astype(wd_ref.dtype)
    return _dot(act, wd_ref[...])


def _ffn_kernel(x_ref, h_ref, wg_ref, wu_ref, wd_ref, g_ref, b_ref, of_ref, ob_ref, acc_ref):
    f = pl.program_id(1)
    part = _swiglu_part(x_ref[...], wg_ref, wu_ref, wd_ref)

    @pl.when(f == 0)
    def _():
        acc_ref[...] = part

    @pl.when(f > 0)
    def _():
        acc_ref[...] += part

    @pl.when(f == pl.num_programs(1) - 1)
    def _():
        out = _layer_norm(ALPHA * h_ref[...] + acc_ref[...], g_ref[...], b_ref[...])
        of_ref[...] = out
        ob_ref[...] = out.astype(BF16)


def ffn_res_ln(x, h, w_gu, w_down, g, b, tm=512, nf=2, lead=()):
    m, d = x.shape
    ff = w_down.shape[-2]
    tf = ff // nf
    tm = _row_tile(m, tm)
    row = lambda i, f: (i, 0)
    fixed = lambda i, f: (0, 0)
    return pl.pallas_call(
        _ffn_kernel,
        grid=(m // tm, nf),
        in_specs=[pl.BlockSpec((tm, d), row), pl.BlockSpec((tm, d), row),
                  _wspec(lead, (d, tf), lambda i, f: (0, f)),
                  _wspec(lead, (d, tf), lambda i, f: (0, nf + f)),
                  _wspec(lead, (tf, d), lambda i, f: (f, 0)),
                  pl.BlockSpec((1, d), fixed), pl.BlockSpec((1, d), fixed)],
        out_specs=[pl.BlockSpec((tm, d), row), pl.BlockSpec((tm, d), row)],
        out_shape=[jax.ShapeDtypeStruct((m, d), F32), jax.ShapeDtypeStruct((m, d), BF16)],
        scratch_shapes=[pltpu.VMEM((tm, d), F32)],
        compiler_params=_params("parallel", "arbitrary"),
        name="ffn_res_ln",
    )(x, h, w_gu, w_gu, w_down, g, b)


def _router_kernel(x_ref, w_ref, dg_ref, meta_ref, cnt_ref, *, n_experts):
    x, w = x_ref[...], w_ref[...]
    x_hi, w_hi = x.astype(BF16), w.astype(BF16)
    x_lo, w_lo = (x - x_hi.astype(F32)).astype(BF16), (w - w_hi.astype(F32)).astype(BF16)
    logits = _dot(x_hi, w_hi) + (_dot(x_lo, w_hi) + _dot(x_hi, w_lo))
    lane = lax.broadcasted_iota(jnp.int32, logits.shape, 1)
    lg = jnp.where(lane < n_experts, logits, -jnp.inf)
    v1 = jnp.max(lg, axis=1, keepdims=True)
    i1 = jnp.min(jnp.where(lg == v1, lane, LANES), axis=1, keepdims=True)
    lg2 = jnp.where(lane == i1, -jnp.inf, lg)
    v2 = jnp.max(lg2, axis=1, keepdims=True)
    i2 = jnp.min(jnp.where(lg2 == v2, lane, LANES), axis=1, keepdims=True)
    e = jnp.exp(v2 - v1)
    g1 = 1.0 / (1.0 + e)
    g2 = e / (1.0 + e)
    pick1, pick2 = lane == i1, lane == i2
    dg_ref[...] = jnp.where(pick1, g1, 0.0) + jnp.where(pick2, g2, 0.0)

    rows = logits.shape[0]
    onehot = jnp.logical_or(pick1, pick2).astype(BF16)
    r = lax.broadcasted_iota(jnp.int32, (rows, rows), 0)
    c = lax.broadcasted_iota(jnp.int32, (rows, rows), 1)
    rank = _dot((c < r).astype(BF16), onehot)
    cnt = jnp.sum(onehot.astype(F32), axis=0, keepdims=True)
    padded = jnp.floor((cnt + (BF16_ROWS - 1)) * (1.0 / BF16_ROWS)) * BF16_ROWS
    er = lax.broadcasted_iota(jnp.int32, (LANES, LANES), 0)
    ec = lax.broadcasted_iota(jnp.int32, (LANES, LANES), 1)
    padded_rows = jnp.broadcast_to(padded, (BF16_ROWS, LANES)).astype(BF16)
    start = _dot(padded_rows, (er < ec).astype(BF16))[0:1, :]
    sorted_row = start + rank
    row1 = jnp.sum(jnp.where(pick1, sorted_row, 0.0), axis=1, keepdims=True)
    row2 = jnp.sum(jnp.where(pick2, sorted_row, 0.0), axis=1, keepdims=True)
    meta_ref[...] = (jnp.where(lane == 0, row1, 0.0) + jnp.where(lane == 1, row2, 0.0)
                     + jnp.where(lane == 2, g1, 0.0) + jnp.where(lane == 3, g2, 0.0))
    cnt_ref[0] = jnp.broadcast_to(cnt, (SUBLANES, LANES))


def router(h, w_router, tm=MOE_BLOCK):
    m, d = h.shape
    e = w_router.shape[1]
    w = jnp.pad(w_router, ((0, 0), (0, LANES - e)))
    tm = _row_tile(m, tm)
    row = lambda i: (i, 0)
    return pl.pallas_call(
        functools.partial(_router_kernel, n_experts=e),
        grid=(m // tm,),
        in_specs=[pl.BlockSpec((tm, d), row), pl.BlockSpec((d, LANES), lambda i: (0, 0))],
        out_specs=[pl.BlockSpec((tm, LANES), row), pl.BlockSpec((tm, LANES), row),
                   pl.BlockSpec((1, SUBLANES, LANES), lambda i: (i, 0, 0))],
        out_shape=[jax.ShapeDtypeStruct((m, LANES), F32), jax.ShapeDtypeStruct((m, LANES), F32),
                   jax.ShapeDtypeStruct((m // tm, SUBLANES, LANES), F32)],
        compiler_params=_params("parallel"),
        name="router",
    )(h, w)


def _moe_dense_kernel(x_ref, h_ref, dg_ref, wg_ref, wu_ref, wd_ref, g_ref, b_ref, of_ref, ob_ref, acc_ref):
    e = pl.program_id(1)
    f = pl.program_id(2)
    dg = dg_ref[...]
    lane = lax.broadcasted_iota(jnp.int32, dg.shape, 1)
    w_e = jnp.sum(jnp.where(lane == e, dg, 0.0), axis=1, keepdims=True)
    part = w_e * _swiglu_part(x_ref[...], wg_ref, wu_ref, wd_ref)
    first = jnp.logical_and(e == 0, f == 0)

    @pl.when(first)
    def _():
        acc_ref[...] = part

    @pl.when(jnp.logical_not(first))
    def _():
        acc_ref[...] += part

    @pl.when(jnp.logical_and(e == pl.num_programs(1) - 1, f == pl.num_programs(2) - 1))
    def _():
        out = _layer_norm(ALPHA * h_ref[...] + acc_ref[...], g_ref[...], b_ref[...])
        of_ref[...] = out
        ob_ref[...] = out.astype(BF16)


def moe_dense_res_ln(x, h, dg, w_gu, w_down, g, b, tm=512, nf=2, lead=()):
    m, d = x.shape
    n_e, ff, _ = w_down.shape[-3:]
    tf = ff // nf
    tm = _row_tile(m, tm)
    row = lambda i, e, f: (i, 0)
    fixed = lambda i, e, f: (0, 0)
    return pl.pallas_call(
        _moe_dense_kernel,
        grid=(m // tm, n_e, nf),
        in_specs=[pl.BlockSpec((tm, d), row), pl.BlockSpec((tm, d), row), pl.BlockSpec((tm, LANES), row),
                  _wspec(lead, (None, d, tf), lambda i, e, f: (e, 0, f)),
                  _wspec(lead, (None, d, tf), lambda i, e, f: (e, 0, nf + f)),
                  _wspec(lead, (None, tf, d), lambda i, e, f: (e, f, 0)),
                  pl.BlockSpec((1, d), fixed), pl.BlockSpec((1, d), fixed)],
        out_specs=[pl.BlockSpec((tm, d), row), pl.BlockSpec((tm, d), row)],
        out_shape=[jax.ShapeDtypeStruct((m, d), F32), jax.ShapeDtypeStruct((m, d), BF16)],
        scratch_shapes=[pltpu.VMEM((tm, d), F32)],
        compiler_params=_params("parallel", "arbitrary", "arbitrary"),
        name="moe_dense_res_ln",
    )(x, h, dg, w_gu, w_gu, w_down, g, b)


def _moe_plan(cnt, n_chunks_max, n_tiles):
    nb, n_e = cnt.shape
    cpt = MOE_BLOCK // BF16_ROWS
    chunks = (cnt + BF16_ROWS - 1) // BF16_ROWS
    seg_local = jnp.cumsum(chunks, axis=1) - chunks
    n_chunks = jnp.sum(chunks, axis=1)
    tiles_e = (jnp.sum(chunks, axis=0) + cpt - 1) // cpt
    tile_end = jnp.cumsum(tiles_e)
    seg_global = ((tile_end - tiles_e) * cpt)[None, :] + jnp.cumsum(chunks, axis=0) - chunks
    c = jnp.arange(n_chunks_max, dtype=jnp.int32)
    e_of_c = jnp.sum(c[None, :, None] >= (seg_local + chunks)[:, None, :], axis=2)
    e_of_c = jnp.minimum(e_of_c, n_e - 1).astype(jnp.int32)
    dst = (jnp.take_along_axis(seg_global, e_of_c, axis=1) + c[None, :]
           - jnp.take_along_axis(seg_local, e_of_c, axis=1))
    tile_expert = jnp.sum(jnp.arange(n_tiles, dtype=jnp.int32)[:, None] >= tile_end[None, :], axis=1)
    tile_expert = jnp.minimum(tile_expert, n_e - 1).astype(jnp.int32)
    return (dst.reshape(-1).astype(jnp.int32), n_chunks.astype(jnp.int32), tile_expert,
            tile_end[-1:].astype(jnp.int32))


def _chunk_copy(src_ref, src_chunk, dst_ref, dst_chunk, sem):
    src = src_ref.at[pl.ds(pl.multiple_of(src_chunk * BF16_ROWS, BF16_ROWS), BF16_ROWS), :]
    dst = dst_ref.at[pl.ds(pl.multiple_of(dst_chunk * BF16_ROWS, BF16_ROWS), BF16_ROWS), :]
    return pltpu.make_async_copy(src, dst, sem)


def _dispatch_kernel(dst_ref, nch_ref, x_ref, rows_ref, buf_in_ref, xs_ref, sorted_ref, sem, *, n_chunks_max):
    del buf_in_ref
    b = pl.program_id(0)
    rows = rows_ref[0]
    n_sorted = sorted_ref.shape[0]
    r = lax.broadcasted_iota(jnp.int32, (n_sorted, rows.shape[1]), 0).astype(F32)
    sel = jnp.logical_or(r == rows[0:1, :], r == rows[1:2, :]).astype(BF16)
    sorted_ref[...] = _dot(sel, x_ref[...]).astype(BF16)
    n = nch_ref[b]

    def copy(c):
        return _chunk_copy(sorted_ref, c, xs_ref, dst_ref[b * n_chunks_max + c], sem)

    def start(c, carry):
        copy(c).start()
        return carry

    def wait(c, carry):
        copy(c).wait()
        return carry

    lax.fori_loop(0, n, start, 0)
    lax.fori_loop(0, n, wait, 0)


def _moe_ffn_kernel(te_ref, nt_ref, x_ref, wg_ref, wu_ref, wd_ref, o_ref, acc_ref):
    del te_ref
    i = pl.program_id(0)
    f = pl.program_id(1)

    @pl.when(i < nt_ref[0])
    def _():
        part = _swiglu_part(x_ref[...], wg_ref, wu_ref, wd_ref)

        @pl.when(f == 0)
        def _():
            acc_ref[...] = part

        @pl.when(f > 0)
        def _():
            acc_ref[...] += part

        @pl.when(f == pl.num_programs(1) - 1)
        def _():
            o_ref[...] = acc_ref[...].astype(o_ref.dtype)

    @pl.when(i >= nt_ref[0])
    def _():
        o_ref[...] = jnp.zeros(o_ref.shape, o_ref.dtype)


def _combine_kernel(dst_ref, nch_ref, ys_ref, meta_ref, h_ref, g_ref, b_ref, of_ref, ob_ref, sorted_ref, sem,
                    *, n_chunks_max):
    blk = pl.program_id(0)
    n = nch_ref[blk]

    def copy(c):
        return _chunk_copy(ys_ref, dst_ref[blk * n_chunks_max + c], sorted_ref, c, sem)

    def start(c, carry):
        copy(c).start()
        return carry

    def clear(c, carry):
        sorted_ref[pl.ds(pl.multiple_of(c * BF16_ROWS, BF16_ROWS), BF16_ROWS), :] = jnp.zeros(
            (BF16_ROWS, sorted_ref.shape[1]), sorted_ref.dtype)
        return carry

    def wait(c, carry):
        copy(c).wait()
        return carry

    lax.fori_loop(0, n, start, 0)
    lax.fori_loop(n, n_chunks_max, clear, 0)
    lax.fori_loop(0, n, wait, 0)
    meta = meta_ref[...]
    col = lax.broadcasted_iota(jnp.int32, (meta.shape[0], sorted_ref.shape[0]), 1).astype(F32)
    ys = sorted_ref[...]
    y = (meta[:, 2:3] * _dot((col == meta[:, 0:1]).astype(BF16), ys)
         + meta[:, 3:4] * _dot((col == meta[:, 1:2]).astype(BF16), ys))
    out = _layer_norm(ALPHA * h_ref[...] + y, g_ref[...], b_ref[...])
    of_ref[...] = out
    ob_ref[...] = out.astype(BF16)


def moe_sorted_res_ln(x, h, meta, cnt, w_gu, w_down, g, b, nf=2, lead=()):
    m, d = x.shape
    n_e, ff, _ = w_down.shape[-3:]
    tf = ff // nf
    nb = m // MOE_BLOCK
    n_chunks_max = 2 * MOE_BLOCK // BF16_ROWS + n_e
    n_sorted = n_chunks_max * BF16_ROWS
    cpt = MOE_BLOCK // BF16_ROWS
    n_tiles = -(-nb * n_chunks_max // cpt) + n_e
    dst, n_chunks, tile_expert, n_used = _moe_plan(cnt[:, 0, :n_e].astype(jnp.int32), n_chunks_max, n_tiles)
    rows_t = jnp.swapaxes(meta[:, :SUBLANES].reshape(nb, MOE_BLOCK, SUBLANES), 1, 2)

    any_space = pl.BlockSpec(memory_space=pl.ANY)
    xs = pl.pallas_call(
        functools.partial(_dispatch_kernel, n_chunks_max=n_chunks_max),
        grid_spec=pltpu.PrefetchScalarGridSpec(
            num_scalar_prefetch=2,
            grid=(nb,),
            in_specs=[pl.BlockSpec((MOE_BLOCK, d), lambda i, *_: (i, 0)),
                      pl.BlockSpec((1, SUBLANES, MOE_BLOCK), lambda i, *_: (i, 0, 0)),
                      any_space],
            out_specs=any_space,
            scratch_shapes=[pltpu.VMEM((n_sorted, d), BF16), pltpu.SemaphoreType.DMA(())],
        ),
        out_shape=jax.ShapeDtypeStruct((n_tiles * MOE_BLOCK, d), BF16),
        input_output_aliases={4: 0},
        compiler_params=_params("arbitrary"),
        name="moe_dispatch",
    )(dst, n_chunks, x, rows_t, jnp.zeros((n_tiles * MOE_BLOCK, d), BF16))

    def tile(i, f, te, nt):
        return jnp.maximum(jnp.minimum(i, nt[0] - 1), 0)

    ys = pl.pallas_call(
        _moe_ffn_kernel,
        grid_spec=pltpu.PrefetchScalarGridSpec(
            num_scalar_prefetch=2,
            grid=(n_tiles, nf),
            in_specs=[pl.BlockSpec((MOE_BLOCK, d), lambda i, f, te, nt: (tile(i, f, te, nt), 0)),
                      _wspec(lead, (None, d, tf), lambda i, f, te, nt: (te[tile(i, f, te, nt)], 0, f)),
                      _wspec(lead, (None, d, tf), lambda i, f, te, nt: (te[tile(i, f, te, nt)], 0, nf + f)),
                      _wspec(lead, (None, tf, d), lambda i, f, te, nt: (te[tile(i, f, te, nt)], f, 0))],
            out_specs=pl.BlockSpec((MOE_BLOCK, d), lambda i, f, te, nt: (i, 0)),
            scratch_shapes=[pltpu.VMEM((MOE_BLOCK, d), F32)],
        ),
        out_shape=jax.ShapeDtypeStruct((n_tiles * MOE_BLOCK, d), BF16),
        compiler_params=_params("arbitrary", "arbitrary"),
        name="moe_expert_ffn",
    )(tile_expert, n_used, xs, w_gu, w_gu, w_down)

    row = lambda i, *_: (i, 0)
    fixed = lambda i, *_: (0, 0)
    return pl.pallas_call(
        functools.partial(_combine_kernel, n_chunks_max=n_chunks_max),
        grid_spec=pltpu.PrefetchScalarGridSpec(
            num_scalar_prefetch=2,
            grid=(nb,),
            in_specs=[any_space, pl.BlockSpec((MOE_BLOCK, LANES), row), pl.BlockSpec((MOE_BLOCK, d), row),
                      pl.BlockSpec((1, d), fixed), pl.BlockSpec((1, d), fixed)],
            out_specs=[pl.BlockSpec((MOE_BLOCK, d), row), pl.BlockSpec((MOE_BLOCK, d), row)],
            scratch_shapes=[pltpu.VMEM((n_sorted, d), BF16), pltpu.SemaphoreType.DMA(())],
        ),
        out_shape=[jax.ShapeDtypeStruct((m, d), F32), jax.ShapeDtypeStruct((m, d), BF16)],
        compiler_params=_params("arbitrary"),
        name="moe_combine_res_ln",
    )(dst, n_chunks, ys, meta, h, g, b)


def _log_sigmoid(x):
    return jnp.minimum(x, 0.0) - jnp.log(1.0 + jnp.exp(-jnp.abs(x)))


def _mlstm_kernel(q_ref, k_ref, v_ref, o_ref, gc_ref, gr_ref, bc_ref, br_ref, gh_ref, c0_ref, n0_ref, m0_ref,
                  hg_ref, c_ref, n_ref, m_ref, *, chunk, t_valid, nh, dh):
    ci = pl.program_id(1)

    @pl.when(ci == 0)
    def _():
        c_ref[...] = c0_ref[...]
        n_ref[...] = n0_ref[...]
        m_ref[...] = m0_ref[...]

    gc = gc_ref[0] + bc_ref[...]
    gr = gr_ref[0] + br_ref[...]
    li_c, lf_c = gc[:, :nh], _log_sigmoid(gc[:, nh:])
    li_r, lf_r = gr[:nh, :], _log_sigmoid(gr[nh:, :])
    if t_valid < chunk:
        tok_c = lax.broadcasted_iota(jnp.int32, li_c.shape, 0) < t_valid
        tok_r = lax.broadcasted_iota(jnp.int32, li_r.shape, 1) < t_valid
        li_c, lf_c = jnp.where(tok_c, li_c, NEG), jnp.where(tok_c, lf_c, 0.0)
        li_r, lf_r = jnp.where(tok_r, li_r, NEG), jnp.where(tok_r, lf_r, 0.0)
    row = lax.broadcasted_iota(jnp.int32, (chunk, chunk), 0)
    col = lax.broadcasted_iota(jnp.int32, (chunk, chunk), 1)
    causal = col <= row
    hi = lax.Precision.HIGHEST
    b_c = jnp.dot(causal.astype(F32), lf_c, preferred_element_type=F32, precision=hi)
    b_r = jnp.dot(lf_r, (row <= col).astype(F32), preferred_element_type=F32, precision=hi)

    mm = q_ref.dtype
    staged = []
    for h in range(nh):
        hs = slice(h * dh, (h + 1) * dh)
        bcol, icol = b_c[:, h:h + 1], li_c[:, h:h + 1]
        brow, irow = b_r[h:h + 1, :], li_r[h:h + 1, :]
        m_prev = m_ref[0, h, :, 0:1]
        c_prev = c_ref[0, h]
        q = q_ref[:, hs]
        kf = k_ref[:, hs].astype(F32) * (dh ** -0.5)
        dmat = jnp.where(causal, bcol - brow + irow, NEG)
        inter = bcol + m_prev
        mt = jnp.maximum(inter, jnp.max(dmat, axis=1, keepdims=True))
        w_inter = jnp.exp(inter - mt)
        qk = _dot_nt(q, kf.astype(mm))
        qc = _dot(q, c_prev.astype(mm))
        qn = jnp.sum(q.astype(F32) * n_ref[0, h], axis=1, keepdims=True)
        staged.append((hs, bcol, icol, m_prev, c_prev, q, kf, dmat, mt, w_inter, qk, qc, qn))

    for h, (hs, bcol, icol, m_prev, c_prev, q, kf, dmat, mt, w_inter, qk, qc, qn) in enumerate(staged):
        n_prev = n_ref[0, h]
        v = v_ref[:, hs]
        b_last = bcol[chunk - 1:chunk, :]
        gdec = b_last - bcol + icol
        m_new = jnp.maximum(b_last + m_prev, jnp.max(gdec, axis=0, keepdims=True))
        decay = jnp.exp(b_last + m_prev - m_new)
        kw = kf * jnp.exp(gdec - m_new)
        c_ref[0, h] = decay * c_prev + _dot_tn(kw.astype(mm), v)
        n_ref[0, h] = decay * n_prev + jnp.sum(kw, axis=0, keepdims=True)
        m_ref[0, h] = jnp.broadcast_to(m_new, (1, LANES))

    readout = []
    for h, (hs, bcol, icol, m_prev, c_prev, q, kf, dmat, mt, w_inter, qk, qc, qn) in enumerate(staged):
        s = qk * jnp.exp(dmat - mt)
        num = w_inter * qc + _dot(s.astype(mm), v_ref[:, hs])
        den = w_inter * qn + jnp.sum(s, axis=1, keepdims=True)
        readout.append((num, den))

    for h, (num, den) in enumerate(readout):
        hs, mt = staged[h][0], staged[h][8]
        hh = num * (1.0 / jnp.maximum(jnp.abs(den), jnp.exp(-mt)))
        mu = jnp.mean(hh, axis=1, keepdims=True)
        hc = hh - mu
        var = jnp.mean(hc * hc, axis=1, keepdims=True)
        hn = hc * lax.rsqrt(var + LN_EPS) * gh_ref[h:h + 1, :]
        hg_ref[:, hs] = (hn * _sigmoid(o_ref[:, hs])).astype(hg_ref.dtype)


def mlstm(qkv, o, gates, b_gate, g_head, c0, n0, m0, t_valid, chunk):
    bsz, t_len, d3 = qkv.shape
    d = d3 // 3
    nh = g_head.shape[0]
    dh = d // nh
    nc = t_len // chunk
    gates_t = jnp.swapaxes(gates, 1, 2)
    seq = lambda b, c: (b, 0, 0, 0)
    fixed = lambda b, c: (0, 0)
    kern = functools.partial(_mlstm_kernel, chunk=chunk, t_valid=t_valid, nh=nh, dh=dh)
    hg, c, n, m = pl.pallas_call(
        kern,
        grid=(bsz, nc),
        in_specs=[pl.BlockSpec((None, chunk, d), lambda b, c: (b, c, 0)),
                  pl.BlockSpec((None, chunk, d), lambda b, c: (b, c, 1)),
                  pl.BlockSpec((None, chunk, d), lambda b, c: (b, c, 2)),
                  pl.BlockSpec((None, chunk, d), lambda b, c: (b, c, 0)),
                  pl.BlockSpec((1, chunk, 2 * nh), lambda b, c: (b, c, 0)),
                  pl.BlockSpec((1, 2 * nh, chunk), lambda b, c: (b, 0, c)),
                  pl.BlockSpec((1, 2 * nh), fixed), pl.BlockSpec((2 * nh, 1), fixed),
                  pl.BlockSpec((nh, dh), fixed),
                  pl.BlockSpec((1, nh, dh, dh), seq), pl.BlockSpec((1, nh, 1, dh), seq),
                  pl.BlockSpec((1, nh, 1, LANES), seq)],
        out_specs=[pl.BlockSpec((None, chunk, d), lambda b, c: (b, c, 0)),
                   pl.BlockSpec((1, nh, dh, dh), seq), pl.BlockSpec((1, nh, 1, dh), seq),
                   pl.BlockSpec((1, nh, 1, LANES), seq)],
        out_shape=[jax.ShapeDtypeStruct((bsz, t_len, d), qkv.dtype),
                   jax.ShapeDtypeStruct((bsz, nh, dh, dh), F32),
                   jax.ShapeDtypeStruct((bsz, nh, 1, dh), F32),
                   jax.ShapeDtypeStruct((bsz, nh, 1, LANES), F32)],
        compiler_params=_params("parallel", "arbitrary"),
        name="mlstm",
    )(qkv, qkv, qkv, o, gates, gates_t, b_gate.reshape(1, 2 * nh), b_gate.reshape(2 * nh, 1), g_head,
      c0, n0.reshape(bsz, nh, 1, dh), jnp.broadcast_to(m0[:, :, None, None], (bsz, nh, 1, LANES)))
    return hg, c, n.reshape(bsz, nh, dh), m[:, :, 0, 0]


def _lambda_value(lam_ref, lam_init):
    lp = lam_ref[...]
    a = jnp.sum(lp[0:1, :] * lp[1:2, :], axis=1, keepdims=True)
    b = jnp.sum(lp[2:3, :] * lp[3:4, :], axis=1, keepdims=True)
    return jnp.exp(a) - jnp.exp(b) + lam_init


def _split_queries(q, dh):
    lane = lax.broadcasted_iota(jnp.int32, q.shape, 1)
    zero = jnp.zeros_like(q)
    return jnp.concatenate([jnp.where(lane < dh, q, zero), jnp.where(lane >= dh, q, zero)], axis=0)


def _flash_step(qi, q_ref, k_ref, vt_ref, lam_ref, g_ref, o_ref, qs_ref, s_ref, m_ref, l_ref, acc_ref,
                *, blk, dh, lam_init, interleave=()):
    qs_ref[...] = _split_queries(q_ref[...], dh)
    m_ref[...] = jnp.full(m_ref.shape, NEG, F32)
    l_ref[...] = jnp.zeros(l_ref.shape, F32)
    acc_ref[...] = jnp.zeros(acc_ref.shape, F32)
    sub = 2 * blk // ATTN_QSPLIT

    def scores(ki, c):
        k = k_ref[pl.ds(pl.multiple_of(ki * blk, blk), blk), :]
        return _dot_nt(k, qs_ref[c * sub:(c + 1) * sub, :])

    def update(ki, c, s, diagonal):
        cs = slice(c * sub, (c + 1) * sub)
        vt = vt_ref[:, pl.ds(pl.multiple_of(ki * blk, blk), blk)]
        if diagonal:
            key = lax.broadcasted_iota(jnp.int32, s.shape, 0)
            qry = (lax.broadcasted_iota(jnp.int32, s.shape, 1) + c * sub) % blk
            s = jnp.where(key <= qry, s, NEG)
        m_prev = m_ref[:, cs]
        m_new = jnp.maximum(m_prev, jnp.max(s, axis=0, keepdims=True))
        alpha = jnp.exp2(m_prev - m_new)
        p = jnp.exp2(s - m_new)
        l_ref[:, cs] = alpha * l_ref[:, cs] + jnp.sum(p, axis=0, keepdims=True)
        acc_ref[:, cs] = alpha * acc_ref[:, cs] + _dot(vt, p.astype(BF16))
        m_ref[:, cs] = m_new

    def block(ki, diagonal):
        pending = [s_ref[i] for i in range(ATTN_LOOKAHEAD)]
        for c in range(ATTN_QSPLIT):
            nxt = c + ATTN_LOOKAHEAD
            if nxt < ATTN_QSPLIT:
                pending.append(scores(ki, nxt))
            elif not diagonal:
                pending.append(scores(ki + 1, nxt - ATTN_QSPLIT))
            if diagonal and c < len(interleave):
                interleave[c]()
            update(ki, c, pending.pop(0), diagonal)
        if diagonal:
            for extra in interleave[ATTN_QSPLIT:]:
                extra()
        for i, s in enumerate(pending):
            s_ref[i] = s

    def body(ki, carry):
        block(ki, False)
        return carry

    for i in range(ATTN_LOOKAHEAD):
        s_ref[i] = scores(0, i)
    lax.fori_loop(0, qi, body, 0)
    block(qi, True)
    lam = _lambda_value(lam_ref, lam_init)
    acc = acc_ref[...]
    r = 1.0 / l_ref[...]
    o = acc[:, :blk] * r[:, :blk] - lam * (acc[:, blk:] * r[:, blk:])
    ms = jnp.mean(o * o, axis=0, keepdims=True)
    o = o * lax.rsqrt(ms + LN_EPS) * (g_ref[...] * (1.0 - lam_init))
    o_ref[...] = o.T.astype(o_ref.dtype)


def _decode_stages(j, n_steps, q_ref, k_refs, v_refs, kn_ref, vn_ref, lam_ref, g_ref, o_ref, qs_ref, m_ref, l_ref,
                   acc_ref, s_ref, p_ref, alpha_ref, *, nh, dh, lam_init):
    page = k_refs[0].shape[0]
    rows = page * nh

    def begin():
        @pl.when(j == 0)
        def _():
            qs = _split_queries(q_ref[0], dh) * (dh ** -0.5)
            qs_ref[...] = qs.astype(BF16)
            kn = jnp.concatenate([kn_ref[0, 0], kn_ref[0, 0]], axis=0)
            m_ref[...] = jnp.sum(qs * kn, axis=1, keepdims=True)
            l_ref[...] = jnp.ones(l_ref.shape, F32)
            acc_ref[...] = jnp.concatenate([vn_ref[0, 0], vn_ref[0, 0]], axis=0)

    def scores_of(first, count):
        def run():
            if count == 0:
                return
            qs = qs_ref[...]
            s = jnp.concatenate([_dot_nt(qs, k_ref[...].reshape(rows, 2 * dh).astype(BF16))
                                 for k_ref in k_refs[first:first + count]], axis=1)
            row = lax.broadcasted_iota(jnp.int32, s.shape, 0) % nh
            col = lax.broadcasted_iota(jnp.int32, s.shape, 1) % nh
            s_ref[:, first * rows:(first + count) * rows] = jnp.where(row == col, s, NEG)
        return run

    half = (len(k_refs) + 1) // 2
    scores_lo, scores_hi = scores_of(0, half), scores_of(half, len(k_refs) - half)

    def softmax():
        s = s_ref[...]
        m_prev = m_ref[...]
        m_new = jnp.maximum(m_prev, jnp.max(s, axis=1, keepdims=True))
        alpha = jnp.exp(m_prev - m_new)
        p = jnp.exp(s - m_new).astype(BF16)
        l_ref[...] = alpha * l_ref[...] + jnp.sum(p.astype(F32), axis=1, keepdims=True)
        m_ref[...] = m_new
        alpha_ref[...] = alpha
        p_ref[...] = p

    def accumulate():
        pv = _dot(p_ref[:, :rows], v_refs[0][...].reshape(rows, 2 * dh).astype(BF16))
        for i in range(1, len(v_refs)):
            pv += _dot(p_ref[:, i * rows:(i + 1) * rows], v_refs[i][...].reshape(rows, 2 * dh).astype(BF16))
        acc_ref[...] = alpha_ref[...] * acc_ref[...] + pv

    def finish():
        @pl.when(j == n_steps - 1)
        def _():
            lam = _lambda_value(lam_ref, lam_init)
            acc = acc_ref[...]
            r = 1.0 / l_ref[...]
            o = acc[:nh] * r[:nh] - lam * (acc[nh:] * r[nh:])
            ms = jnp.mean(o * o, axis=1, keepdims=True)
            o_ref[0] = o * lax.rsqrt(ms + LN_EPS) * g_ref[...] * (1.0 - lam_init)

    return begin, scores_lo, scores_hi, softmax, accumulate, finish


def _attention_kernel(pt_ref, q_ref, k_ref, vt_ref, lam_ref, gcol_ref, qd_ref, ck_ref, cv_ref, kn_ref, vn_ref,
                      grow_ref, o_ref, od_ref, qs_ref, s_ref, m_ref, l_ref, acc_ref, dqs_ref, dm_ref, dl_ref,
                      dacc_ref, ds_ref, dp_ref, dalpha_ref, kbuf_ref, vbuf_ref, sem,
                      *, blk, dh, nh, lam_init, pps, steps_per_seq, n_decode_steps, n_grid_steps):
    step = (pl.program_id(0) * pl.num_programs(1) + pl.program_id(1)) * pl.num_programs(2) + pl.program_id(2)
    slot = step % 2

    def page_copies(t, into):
        seq, first = t // steps_per_seq, (t % steps_per_seq) * pps
        for p in range(pps):
            phys = pt_ref[seq, first + p]
            yield pltpu.make_async_copy(ck_ref.at[phys], kbuf_ref.at[into, p], sem.at[into])
            yield pltpu.make_async_copy(cv_ref.at[phys], vbuf_ref.at[into, p], sem.at[into])

    @pl.when(step == 0)
    def _():
        for copy in page_copies(step, slot):
            copy.start()

    @pl.when(step + 1 < n_decode_steps)
    def _():
        for copy in page_copies(step + 1, 1 - slot):
            copy.start()

    @pl.when(step < n_decode_steps)
    def _():
        for copy in page_copies(step, slot):
            copy.wait()

    kd_refs = [kbuf_ref.at[slot, p] for p in range(pps)]
    vd_refs = [vbuf_ref.at[slot, p] for p in range(pps)]
    stages = _decode_stages(step % steps_per_seq, steps_per_seq, qd_ref, kd_refs, vd_refs, kn_ref, vn_ref, lam_ref,
                            grow_ref, od_ref, dqs_ref, dm_ref, dl_ref, dacc_ref, ds_ref, dp_ref, dalpha_ref,
                            nh=nh, dh=dh, lam_init=lam_init)
    if n_decode_steps != n_grid_steps:
        stages = [functools.partial(pl.when(step < n_decode_steps), f) for f in stages]
    begin, scores_lo, scores_hi, softmax, accumulate, finish = stages

    def scores_hi_softmax():
        scores_hi()
        softmax()

    begin()
    _flash_step(pl.program_id(2), q_ref, k_ref, vt_ref, lam_ref, gcol_ref, o_ref, qs_ref, s_ref, m_ref, l_ref,
                acc_ref, blk=blk, dh=dh, lam_init=lam_init, interleave=(scores_lo, scores_hi_softmax, accumulate))
    finish()


def diff_attention(q, k, vt, q_dec, k_new, v_new, cache_k, cache_v, page_table, lam_p, g_subln, lam_init,
                   blk=ATTN_BLOCK):
    bsz, t_len, d = q.shape
    n_seq, nh, hw = q_dec.shape
    dh = hw // 2
    blk = min(blk, t_len)
    nq = t_len // blk
    n_steps = bsz * nh * nq
    n_pages = page_table.shape[1]
    page = cache_k.shape[1]
    pps = min(p for p in range(1, n_pages + 1) if n_pages % p == 0 and n_seq * (n_pages // p) <= n_steps)
    steps_per_seq = n_pages // pps
    n_decode_steps = n_seq * steps_per_seq

    def decode_pos(b, h, i):
        step = jnp.minimum((b * nh + h) * nq + i, n_decode_steps - 1)
        return step // steps_per_seq, step % steps_per_seq

    def seq3(b, h, i, pt):
        return (decode_pos(b, h, i)[0], 0, 0)

    def seq4(b, h, i, pt):
        return (decode_pos(b, h, i)[0], 0, 0, 0)

    in_hbm = pl.BlockSpec(memory_space=pl.ANY)
    fixed = lambda b, h, i, pt: (0, 0)
    grid_spec = pltpu.PrefetchScalarGridSpec(
        num_scalar_prefetch=1,
        grid=(bsz, nh, nq),
        in_specs=([pl.BlockSpec((None, blk, hw), lambda b, h, i, pt: (b, i, h)),
                   pl.BlockSpec((None, t_len, hw), lambda b, h, i, pt: (b, 0, h)),
                   pl.BlockSpec((None, hw, t_len), lambda b, h, i, pt: (b, h, 0)),
                   pl.BlockSpec(lam_p.shape, fixed), pl.BlockSpec((hw, 1), fixed),
                   pl.BlockSpec((1, nh, hw), seq3), in_hbm, in_hbm,
                   pl.BlockSpec((1, 1, nh, hw), seq4), pl.BlockSpec((1, 1, nh, hw), seq4),
                   pl.BlockSpec((1, hw), fixed)]),
        out_specs=[pl.BlockSpec((None, blk, hw), lambda b, h, i, pt: (b, i, h)),
                   pl.BlockSpec((1, nh, hw), seq3)],
        scratch_shapes=[pltpu.VMEM((2 * blk, hw), BF16),
                        pltpu.VMEM((ATTN_LOOKAHEAD, blk, 2 * blk // ATTN_QSPLIT), F32),
                        pltpu.VMEM((1, 2 * blk), F32), pltpu.VMEM((1, 2 * blk), F32), pltpu.VMEM((hw, 2 * blk), F32),
                        pltpu.VMEM((2 * nh, hw), BF16), pltpu.VMEM((2 * nh, 1), F32),
                        pltpu.VMEM((2 * nh, 1), F32), pltpu.VMEM((2 * nh, hw), F32),
                        pltpu.VMEM((2 * nh, pps * page * nh), F32), pltpu.VMEM((2 * nh, pps * page * nh), BF16),
                        pltpu.VMEM((2 * nh, 1), F32),
                        pltpu.VMEM((2, pps, page, nh, hw), F32), pltpu.VMEM((2, pps, page, nh, hw), F32),
                        pltpu.SemaphoreType.DMA((2,))],
    )
    return pl.pallas_call(
        functools.partial(_attention_kernel, blk=blk, dh=dh, nh=nh, lam_init=lam_init, pps=pps,
                          steps_per_seq=steps_per_seq, n_decode_steps=n_decode_steps, n_grid_steps=n_steps),
        grid_spec=grid_spec,
        out_shape=[jax.ShapeDtypeStruct((bsz, t_len, d), BF16), jax.ShapeDtypeStruct((n_seq, nh, hw), F32)],
        compiler_params=_params("arbitrary", "arbitrary", "arbitrary"),
        name="diff_attention",
    )(page_table, q, k, vt, lam_p, g_subln.reshape(hw, 1), q_dec, cache_k, cache_v, k_new, v_new,
      g_subln.reshape(1, hw))


def _trunk(x, state, decode, wts):
    bsz, t_len, d = x.shape
    m_rows = bsz * t_len
    n_a = wts["w_in"].shape[0]
    hw = wts["g_subln"].shape[-1]
    n_heads = d // hw
    act = wts["w_in"].dtype
    pick = (lambda h32, hbf: hbf) if act == BF16 else (lambda h32, hbf: h32)
    h32 = x.reshape(m_rows, d)
    ha = h32.astype(act)
    cs, ns, ms = [], [], []
    k32 = v32 = kbf = vt = None
    for i in range(DEPTH):
        ln_g, ln_b = wts["ln_g"][i], wts["ln_b"][i]
        if i < n_a:
            (qkv,) = matmul(ha, wts["w_in"], [act], lead=(i,), n=3 * d)
            (og,) = matmul(ha, wts["w_in"], [F32], lead=(i,), col0=3, n=d)
            (gates,) = matmul(ha, wts["w_gates"], [F32], lead=(i,))
            chunk = MLSTM_CHUNK if t_len % MLSTM_CHUNK == 0 else BF16_ROWS
            t_pad = -(-t_len // chunk) * chunk
            gates = gates[:, :2 * A_HEADS].reshape(bsz, t_len, 2 * A_HEADS)
            og = og.reshape(bsz, t_len, d)
            qkv = qkv.reshape(bsz, t_len, 3 * d)
            if t_pad != t_len:
                pad = ((0, 0), (0, t_pad - t_len), (0, 0))
                qkv, og, gates = jnp.pad(qkv, pad), jnp.pad(og, pad), jnp.pad(gates, pad)
            if state is None:
                dh = d // A_HEADS
                c0 = jnp.zeros((bsz, A_HEADS, dh, dh), F32)
                n0 = jnp.zeros((bsz, A_HEADS, dh), F32)
                m0 = jnp.zeros((bsz, A_HEADS), F32)
            else:
                c0, n0, m0 = state[0][i], state[1][i], state[2][i]
            hg, c, n, m = mlstm(qkv, og, gates, wts["b_gate"][i], wts["g_head"][i], c0, n0, m0, t_len, chunk)
            cs.append(c)
            ns.append(n)
            ms.append(m)
            mix_in = hg[:, :t_len].reshape(m_rows, d)
            w_mix, mix_lead = wts["w_out"], (i,)
        else:
            j = i - n_a
            if j == 0:
                if not decode:
                    k32, kbf = matmul(ha, wts["w_kv"], [F32, BF16], n=d, heads=n_heads)
                    (v32,) = matmul(ha, wts["w_kv"], [F32], col0=1, n=d, heads=n_heads)
                    vt = matmul_t(ha.reshape(bsz, t_len, d), wts["w_v_t"])
                else:
                    (k32,) = matmul(ha, wts["w_kv"], [F32], n=d, heads=n_heads)
                    (v32,) = matmul(ha, wts["w_kv"], [F32], col0=1, n=d, heads=n_heads)
            if not decode:
                (q,) = matmul(ha, wts["w_q"], [BF16], lead=(j,), scale=math.log2(math.e) * (hw // 2) ** -0.5)
                o = yield {"q": q.reshape(bsz, t_len, d), "k": kbf.reshape(bsz, t_len, d), "vt": vt}
                mix_in = o.reshape(m_rows, d)
            else:
                (q,) = matmul(ha, wts["w_q"], [F32], lead=(j,))
                o = yield {"q": q.reshape(bsz, n_heads, hw), "k_new": k32.reshape(bsz, 1, n_heads, hw),
                           "v_new": v32.reshape(bsz, 1, n_heads, hw)}
                mix_in = o.reshape(m_rows, d).astype(act)
            w_mix, mix_lead = wts["w_o"], (j,)
        h32, hbf = matmul_res_ln(mix_in, w_mix, h32, ln_g[0:1], ln_b[0:1], lead=mix_lead)
        ha = pick(h32, hbf)
        if i % 2 == 0:
            h32, hbf = ffn_res_ln(ha, h32, wts["w_ffn_gu"], wts["w_ffn_down"], ln_g[1:2], ln_b[1:2], lead=(i // 2,))
        else:
            dg, meta, cnt = router(h32, wts["w_router"][i // 2])
            w_gu, w_down, lead = wts["w_moe_gu"], wts["w_moe_down"], (i // 2,)
            if m_rows >= MOE_MIN_ROWS and m_rows % MOE_BLOCK == 0:
                h32, hbf = moe_sorted_res_ln(hbf, h32, meta, cnt, w_gu, w_down, ln_g[1:2], ln_b[1:2], lead=lead)
            else:
                h32, hbf = moe_dense_res_ln(hbf, h32, dg, w_gu, w_down, ln_g[1:2], ln_b[1:2], lead=lead)
        ha = pick(h32, hbf)
    y = h32.reshape(bsz, t_len, d)
    k_rows = k32.reshape(bsz, t_len, n_heads, hw)
    v_rows = v32.reshape(bsz, t_len, n_heads, hw)
    return y, jnp.stack(cs), jnp.stack(ns), jnp.stack(ms), k_rows, v_rows


def _resume(gen, value):
    try:
        return gen.send(value), None
    except StopIteration as done:
        return None, done.value


def kernel(x_prompt, x_sample, state_c, state_n, state_m, cache_k, cache_v, page_table, ln_g, ln_b, w_in_a, b_gate_a, g_head_a, w_out_a, w_kv, w_q_b, lam_b, g_subln_b, w_o_b, w_ffn_gu, w_ffn_down, w_router, w_moe_gu, w_moe_down):
    d = x_prompt.shape[-1]
    n_gate = w_in_a.shape[-1] - 4 * d
    w_gates = jnp.pad(w_in_a[:, :, 4 * d:], ((0, 0), (0, 0), (0, LANES - n_gate)))

    moe_bf16 = {"w_moe_gu": w_moe_gu.astype(BF16), "w_moe_down": w_moe_down.astype(BF16)}

    def weights(cast):
        return {
            "ln_g": ln_g, "ln_b": ln_b, "b_gate": b_gate_a, "g_head": g_head_a, "w_router": w_router,
            "g_subln": g_subln_b,
            "w_in": cast(w_in_a), "w_gates": cast(w_gates), "w_out": cast(w_out_a), "w_kv": cast(w_kv),
            "w_v_t": cast(w_kv[:, d:].T), "w_q": cast(w_q_b), "w_o": cast(w_o_b), "w_ffn_gu": cast(w_ffn_gu),
            "w_ffn_down": cast(w_ffn_down), **moe_bf16,
        }

    prompt = _trunk(x_prompt, None, False, weights(lambda a: a.astype(BF16)))
    sample = _trunk(x_sample, (state_c, state_n, state_m), True, weights(lambda a: a))
    n_a = w_in_a.shape[0]
    req_p, req_s = next(prompt), next(sample)
    out_p = out_s = None
    for j in range(w_q_b.shape[0]):
        o_p, o_s = diff_attention(req_p["q"], req_p["k"], req_p["vt"], req_s["q"], req_s["k_new"], req_s["v_new"],
                                  cache_k, cache_v, page_table, lam_b[j], g_subln_b[j], _lambda_init(n_a + j))
        req_p, out_p = _resume(prompt, o_p)
        req_s, out_s = _resume(sample, o_s)
    y_p, c_p, n_p, m_p, k_p, v_p = out_p
    y_s, c_s, n_s, m_s, k_s, v_s = out_s
    return (y_p, y_s, c_p, n_p, m_p, k_p, v_p, c_s, n_s, m_s, k_s, v_s)
```
